```python
import math
import jax, jax.numpy as jnp
from jax import lax
import numpy as np

D_MODEL = 2048
BATCH = 16
SEQ = 2048
DEPTH = 4
DEC_BATCH = 8
DEC_SEQ = 2048
PAST_LEN = 128

HEAD_DIM = 64
N_BRANCH = 4
BRANCH_W = D_MODEL // N_BRANCH
A_HEADS = BRANCH_W // (2 * HEAD_DIM)
Q_BLOCK = 128
B_HEADS = BRANCH_W // HEAD_DIM
B_KV_HEADS = B_HEADS // 4
B_WINDOW = 128
C_HEADS = BRANCH_W // HEAD_DIM
C_DILATIONS = ((128, 1), (512, 4), (2048, 16))
N_DIL = 3
GRID_W = 64
D_HEADS = BRANCH_W // HEAD_DIM
NA_ROWS = 8
NA_COLS = 16
NA_KEY_COLS = 2 * NA_COLS
A_QK = A_HEADS * 2 * HEAD_DIM
A_V = A_HEADS * 2 * HEAD_DIM
B_Q = B_HEADS * HEAD_DIM
B_KV = B_KV_HEADS * HEAD_DIM
C_W = C_HEADS * HEAD_DIM
D_W = D_HEADS * HEAD_DIM
IN_SPLITS = (A_QK, A_QK, A_V, B_Q, B_KV, B_KV) + (C_W,) * (3 * N_DIL) + (D_W,) * 3
IN_COLS = sum(IN_SPLITS)
N_EXPERTS = 32
TOP_K = 4
D_FF = D_MODEL
SWIGLU_ALPHA = 1.702
SWIGLU_LIMIT = 7.0
MOE_BLOCK = 256
RMS_EPS = 1e-6
NEG_INF = -1e30

kernel_name = 'hybrid_gated_parallel_encoder'


def rms_norm(x, g):
    xf = x.astype(jnp.float32)
    y = xf * lax.rsqrt(jnp.mean(xf * xf, axis=-1, keepdims=True) + RMS_EPS)
    return (y * g.astype(jnp.float32)).astype(x.dtype)


def alibi_slopes(n):
    return 2.0 ** (-8.0 * jnp.arange(1, n + 1, dtype=jnp.float32) / n)


def diff_attention(q, k, v, lam, slopes):
    b, s, h, _, dh = q.shape
    nq = s // Q_BLOCK
    q_blocks = q.reshape(b, nq, Q_BLOCK, h, 2, dh).swapaxes(0, 1)
    k_pos = jnp.arange(s)

    def one_block(args):
        qb, i = args
        sc = jnp.einsum('bqhcd,bkhcd->bhcqk', qb, k).astype(jnp.float32) * dh ** -0.5
        q_pos = i * Q_BLOCK + jnp.arange(Q_BLOCK)
        dist = jnp.abs(q_pos[:, None] - k_pos[None, :]).astype(jnp.float32)
        sc = sc - (slopes[:, None, None] * dist)[None, :, None]
        p = jax.nn.softmax(sc, axis=-1)
        w = (p[:, :, 0] - lam * p[:, :, 1]).astype(v.dtype)
        return jnp.einsum('bhqk,bkhe->bqhe', w, v)

    o = lax.map(one_block, (q_blocks, jnp.arange(nq)))
    return o.swapaxes(0, 1).reshape(b, s, h, v.shape[-1])


def banded_attention(q, k, v, window, block, dist_scale, slopes, sink):
    n, L, hk, g, dh = q.shape
    nb = -(-L // block)
    pad = nb * block - L
    qb = jnp.pad(q, ((0, 0), (0, pad), (0, 0), (0, 0), (0, 0))).reshape(n, nb, block, hk, g, dh)

    def windows(t):
        tp = jnp.pad(t, ((0, 0), (block, block + pad), (0, 0), (0, 0))).reshape(n, nb + 2, block, hk, dh)
        return jnp.concatenate([tp[:, :-2], tp[:, 1:-1], tp[:, 2:]], axis=2)

    kw, vw = windows(k), windows(v)
    sc = jnp.einsum('nbqhgd,nbkhd->nbhgqk', qb, kw).astype(jnp.float32) * dh ** -0.5
    qi = jnp.arange(nb)[:, None] * block + jnp.arange(block)[None]
    ki = jnp.arange(nb)[:, None] * block - block + jnp.arange(3 * block)[None]
    rel = jnp.abs(qi[:, :, None] - ki[:, None, :])
    valid = (rel <= window) & (ki[:, None, :] >= 0) & (ki[:, None, :] < L)
    bias = -slopes[None, :, :, None, None] * (dist_scale * rel).astype(jnp.float32)[:, None, None]
    sc = jnp.where(valid[:, None, None], sc + bias, NEG_INF)
    lse = jax.nn.logsumexp(sc, axis=-1)
    if sink is not None:
        lse = jnp.logaddexp(lse, sink.astype(jnp.float32)[None, None, :, :, None])
    p = jnp.exp(sc - lse[..., None]).astype(v.dtype)
    o = jnp.einsum('nbhgqk,nbkhd->nbqhgd', p, vw).reshape(n, nb * block, hk, g, dh)[:, :L]
    lse = lse.transpose(0, 1, 4, 2, 3).reshape(n, nb * block, hk, g)[:, :L]
    return o, lse


def dilated_group_attention(q, k, v, window, dilation, slopes):
    b, s, h, dh = q.shape
    side = window // (2 * dilation)
    sub = s // dilation

    def to_res(t):
        return t.reshape(b, sub, dilation, h, dh).swapaxes(1, 2).reshape(b * dilation, sub, h, dh)

    o, lse = banded_attention(to_res(q)[:, :, :, None], to_res(k), to_res(v), side, side, dilation,
                              slopes[:, None], None)
    o = o.reshape(b, dilation, sub, h, dh).swapaxes(1, 2).reshape(b, s, h, dh)
    lse = lse.reshape(b, dilation, sub, h).swapaxes(1, 2).reshape(b, s, h)
    return o, lse


def neighbourhood_attention(q, k, v, rpb):
    b, s, h, dh = q.shape
    rows = s // GRID_W
    kr = min(NA_ROWS, rows)
    n_cb = GRID_W // NA_COLS
    r_ar = jnp.arange(rows)
    key_rows = jnp.clip(r_ar - kr // 2, 0, rows - kr)[:, None] + jnp.arange(kr)[None]
    q_cols = jnp.arange(GRID_W).reshape(n_cb, NA_COLS)
    col_start = jnp.clip(q_cols - NA_COLS // 2, 0, GRID_W - NA_COLS)
    key_cols = (jnp.clip(jnp.arange(n_cb) * NA_COLS - NA_COLS // 2, 0, GRID_W - NA_KEY_COLS)[:, None]
                + jnp.arange(NA_KEY_COLS)[None])

    def gather(t):
        return t.reshape(b, rows, GRID_W, h, dh)[:, key_rows[:, None, :, None], key_cols[None, :, None, :]]

    qg = q.reshape(b, rows, n_cb, NA_COLS, h, dh)
    sc = jnp.einsum('brjqhd,brjkchd->brjhqkc', qg, gather(k)).astype(jnp.float32) * dh ** -0.5
    dr = key_rows - r_ar[:, None]
    dc = key_cols[:, None, :] - q_cols[:, :, None]
    col_ok = (key_cols[:, None, :] >= col_start[:, :, None]) & (key_cols[:, None, :] < col_start[:, :, None] + NA_COLS)
    bias = rpb[:, (dr + NA_ROWS - 1)[:, None, None, :, None],
               jnp.clip(dc + NA_COLS - 1, 0, 2 * NA_COLS - 2)[None, :, :, None, :]]
    bias = bias.transpose(1, 2, 0, 3, 4, 5).astype(jnp.float32)
    sc = jnp.where(col_ok[None, None, :, None, :, None, :], sc + bias, NEG_INF)
    p = jax.nn.softmax(sc.reshape(*sc.shape[:-2], kr * NA_KEY_COLS), axis=-1)
    p = p.reshape(sc.shape).astype(v.dtype)
    o = jnp.einsum('brjhqkc,brjkchd->brjqhd', p, gather(v))
    return o.reshape(b, s, h, dh)


def token_mixer(h, lam_init, w_in, a_qk_norm, a_lambda, a_subln, b_qk_norm, b_sink,
                c_qk_norm, d_qk_norm, d_rpb, w_gate, b_gate, w_branch, w_out):
    b, s, d = h.shape
    parts = jnp.split(h @ w_in, np.cumsum(IN_SPLITS)[:-1].tolist(), axis=-1)
    aq = rms_norm(parts[0].reshape(b, s, A_HEADS, 2, HEAD_DIM), a_qk_norm[0])
    ak = rms_norm(parts[1].reshape(b, s, A_HEADS, 2, HEAD_DIM), a_qk_norm[1])
    av = parts[2].reshape(b, s, A_HEADS, 2 * HEAD_DIM)
    lv = a_lambda.astype(jnp.float32)
    lam = jnp.exp(jnp.sum(lv[0] * lv[1])) - jnp.exp(jnp.sum(lv[2] * lv[3])) + lam_init
    o_a = diff_attention(aq, ak, av, lam, alibi_slopes(A_HEADS))
    o_a = (rms_norm(o_a, a_subln) * (1.0 - lam_init)).reshape(b, s, BRANCH_W)
    bq = rms_norm(parts[3].reshape(b, s, B_KV_HEADS, B_HEADS // B_KV_HEADS, HEAD_DIM), b_qk_norm[0])
    bk = rms_norm(parts[4].reshape(b, s, B_KV_HEADS, HEAD_DIM), b_qk_norm[1])
    bv = parts[5].reshape(b, s, B_KV_HEADS, HEAD_DIM)
    o_b, _ = banded_attention(bq, bk, bv, B_WINDOW, B_WINDOW, 1,
                              alibi_slopes(B_HEADS).reshape(B_KV_HEADS, -1), b_sink.reshape(B_KV_HEADS, -1))
    o_b = o_b.reshape(b, s, BRANCH_W)
    c_slopes = alibi_slopes(N_DIL * C_HEADS).reshape(N_DIL, C_HEADS)
    outs, lses = [], []
    for gi, (win, dil) in enumerate(C_DILATIONS):
        cq = rms_norm(parts[6 + 3 * gi].reshape(b, s, C_HEADS, HEAD_DIM), c_qk_norm[gi, 0])
        ck = rms_norm(parts[7 + 3 * gi].reshape(b, s, C_HEADS, HEAD_DIM), c_qk_norm[gi, 1])
        cv = parts[8 + 3 * gi].reshape(b, s, C_HEADS, HEAD_DIM)
        o, lse = dilated_group_attention(cq, ck, cv, win, dil, c_slopes[gi])
        outs.append(o)
        lses.append(lse)
    alpha = jax.nn.softmax(jnp.stack(lses), axis=0)
    o_c = jnp.sum(alpha[..., None].astype(h.dtype) * jnp.stack(outs), axis=0).reshape(b, s, BRANCH_W)
    dq = rms_norm(parts[15].reshape(b, s, D_HEADS, HEAD_DIM), d_qk_norm[0])
    dk = rms_norm(parts[16].reshape(b, s, D_HEADS, HEAD_DIM), d_qk_norm[1])
    dv = parts[17].reshape(b, s, D_HEADS, HEAD_DIM)
    o_d = neighbourhood_attention(dq, dk, dv, d_rpb).reshape(b, s, BRANCH_W)
    branches = jnp.stack([o_a, o_b, o_c, o_d], axis=2)
    proj = jnp.einsum('bsnw,nwd->bsnd', branches, w_branch)
    gates = jax.nn.sigmoid(h @ w_gate + b_gate).reshape(b, s, N_BRANCH, d)
    return jnp.sum(gates * proj, axis=2) @ w_out


def moe_ffn(h, w_router, b_router, w_gate_up, b_gate_up, w_down, b_down):
    b, s, d = h.shape
    n_tok = b * s
    xf = h.reshape(n_tok, d)
    logits = (xf @ w_router + b_router).astype(jnp.float32)
    top_logit, top_idx = lax.top_k(logits, TOP_K)
    probs = jax.nn.softmax(top_logit, axis=-1)
    expert = top_idx.reshape(-1)
    token = jnp.arange(n_tok * TOP_K, dtype=jnp.int32) // TOP_K
    onehot = jax.nn.one_hot(expert, N_EXPERTS, dtype=jnp.int32)
    rank = jnp.take_along_axis(jnp.cumsum(onehot, axis=0), expert[:, None], axis=1)[:, 0] - 1
    padded = (onehot.sum(0) + MOE_BLOCK - 1) // MOE_BLOCK * MOE_BLOCK
    pad_end = jnp.cumsum(padded)
    dest = pad_end[expert] - padded[expert] + rank
    n_blocks = -(-(n_tok * TOP_K) // MOE_BLOCK) + N_EXPERTS
    n_slots = n_blocks * MOE_BLOCK
    slot_tok = jnp.zeros((n_slots,), jnp.int32).at[dest].set(token)
    slot_w = jnp.zeros((n_slots,), jnp.float32).at[dest].set(probs.reshape(-1))
    block_expert = jnp.minimum(jnp.searchsorted(pad_end, jnp.arange(n_blocks) * MOE_BLOCK, side='right'),
                               N_EXPERTS - 1)

    def expert_block(args):
        tok_b, w_b, e = args
        xb = xf[tok_b]
        gu = xb @ w_gate_up[e] + b_gate_up[e]
        g, u = jnp.split(gu, 2, axis=-1)
        g = jnp.minimum(g, SWIGLU_LIMIT)
        u = jnp.clip(u, -SWIGLU_LIMIT, SWIGLU_LIMIT)
        a = (u + 1) * g * jax.nn.sigmoid(SWIGLU_ALPHA * g)
        return (a @ w_down[e] + b_down[e]) * w_b[:, None].astype(h.dtype)

    out = lax.map(expert_block, (slot_tok.reshape(n_blocks, MOE_BLOCK), slot_w.reshape(n_blocks, MOE_BLOCK),
                                 block_expert))
    y = jax.ops.segment_sum(out.reshape(n_slots, d), slot_tok, num_segments=n_tok)
    return y.reshape(b, s, d)


def run_trunk(x, c, w_ada, b_ada, norm_g, w_in, a_qk_norm, a_lambda, a_subln, b_qk_norm, b_sink,
              c_qk_norm, d_qk_norm, d_rpb, w_gate, b_gate, w_branch, w_out,
              w_router, b_router, w_gate_up, b_gate_up, w_down, b_down):
    for l in range(DEPTH):
        mod = (jax.nn.silu(c) @ w_ada[l] + b_ada[l])[:, None, :]
        sh1, sc1, g1, sh2, sc2, g2 = jnp.split(mod, 6, axis=-1)
        lam_init = 0.8 - 0.6 * math.exp(-0.3 * l)
        hm = rms_norm(x, norm_g[l, 0]) * (1 + sc1) + sh1
        x = x + g1 * token_mixer(hm, lam_init, w_in[l], a_qk_norm[l], a_lambda[l], a_subln[l],
                                 b_qk_norm[l], b_sink[l], c_qk_norm[l], d_qk_norm[l], d_rpb[l],
                                 w_gate[l], b_gate[l], w_branch[l], w_out[l])
        hf = rms_norm(x, norm_g[l, 1]) * (1 + sc2) + sh2
        x = x + g2 * moe_ffn(hf, w_router[l], b_router[l], w_gate_up[l], b_gate_up[l], w_down[l], b_down[l])
    return x


def setup_inputs(seed: int = 0) -> dict:
    key = jax.random.key(seed)
    ks = jax.random.split(key, 26)
    L, D, F, E, W = DEPTH, D_MODEL, D_FF, N_EXPERTS, BRANCH_W

    def nrm(i, shape, scale):
        return scale * jax.random.normal(ks[i], shape, jnp.float32)

    return {
        'x_prompt': nrm(0, (BATCH, SEQ, D), 1.0),
        'x_sample': nrm(1, (DEC_BATCH, DEC_SEQ, D), 1.0),
        'c_prompt': nrm(2, (BATCH, D), 1.0),
        'c_sample': nrm(3, (DEC_BATCH, D), 1.0),
        'w_ada': nrm(4, (L, D, 6 * D), 0.5 * D ** -0.5),
        'b_ada': nrm(5, (L, 6 * D), 0.02),
        'norm_g': 1.0 + nrm(6, (L, 2, D), 0.02),
        'w_in': nrm(7, (L, D, IN_COLS), D ** -0.5),
        'a_qk_norm': 1.0 + nrm(8, (L, 2, HEAD_DIM), 0.02),
        'a_lambda': nrm(9, (L, 4, HEAD_DIM), 0.1),
        'a_subln': 1.0 + nrm(10, (L, 2 * HEAD_DIM), 0.02),
        'b_qk_norm': 1.0 + nrm(11, (L, 2, HEAD_DIM), 0.02),
        'b_sink': nrm(12, (L, B_HEADS), 0.5),
        'c_qk_norm': 1.0 + nrm(13, (L, N_DIL, 2, HEAD_DIM), 0.02),
        'd_qk_norm': 1.0 + nrm(14, (L, 2, HEAD_DIM), 0.02),
        'd_rpb': nrm(15, (L, D_HEADS, 2 * NA_ROWS - 1, 2 * NA_COLS - 1), 0.5),
        'w_gate': nrm(16, (L, D, N_BRANCH * D), D ** -0.5),
        'b_gate': nrm(17, (L, N_BRANCH * D), 0.02),
        'w_branch': nrm(18, (L, N_BRANCH, W, D), W ** -0.5),
        'w_out': nrm(19, (L, D, D), D ** -0.5),
        'w_router': nrm(20, (L, D, E), D ** -0.5),
        'b_router': nrm(21, (L, E), 0.01),
        'w_gate_up': nrm(22, (L, E, D, 2 * F), D ** -0.5),
        'b_gate_up': nrm(23, (L, E, 2 * F), 0.02),
        'w_down': nrm(24, (L, E, F, D), F ** -0.5),
        'b_down': nrm(25, (L, E, D), 0.02),
    }


def reference(x_prompt, x_sample, c_prompt, c_sample, w_ada, b_ada, norm_g, w_in, a_qk_norm, a_lambda,
              a_subln, b_qk_norm, b_sink, c_qk_norm, d_qk_norm, d_rpb, w_gate, b_gate, w_branch, w_out,
              w_router, b_router, w_gate_up, b_gate_up, w_down, b_down):
    y_prompt = run_trunk(x_prompt, c_prompt, w_ada, b_ada, norm_g, w_in, a_qk_norm, a_lambda, a_subln,
                         b_qk_norm, b_sink, c_qk_norm, d_qk_norm, d_rpb, w_gate, b_gate, w_branch, w_out,
                         w_router, b_router, w_gate_up, b_gate_up, w_down, b_down)
    y_sample = run_trunk(x_sample, c_sample, w_ada, b_ada, norm_g, w_in, a_qk_norm, a_lambda, a_subln,
                         b_qk_norm, b_sink, c_qk_norm, d_qk_norm, d_rpb, w_gate, b_gate, w_branch, w_out,
                         w_router, b_router, w_gate_up, b_gate_up, w_down, b_down)
    return (y_prompt, y_sample)
```

```python
import functools
import math

import numpy as np
import jax
import jax.numpy as jnp
from jax import lax
from jax.experimental import pallas as pl
from jax.experimental.pallas import tpu as pltpu

F32 = jnp.float32
BF16 = jnp.bfloat16
U32 = jnp.uint32
I32 = jnp.int32

D_MODEL = 2048
DEPTH = 4
HEAD_DIM = 64
N_BRANCH = 4
BRANCH_W = D_MODEL // N_BRANCH
A_HEADS = 4
B_HEADS = 8
B_KV_HEADS = 2
B_WINDOW = 128
C_HEADS = 8
C_DILATIONS = ((128, 1), (512, 4), (2048, 16))
N_DIL = 3
GRID_W = 64
D_HEADS = 8
NA_ROWS = 8
NA_COLS = 16
N_EXPERTS = 32
TOP_K = 4
D_FF = D_MODEL
SWIGLU_ALPHA = 1.702
SWIGLU_LIMIT = 7.0
RMS_EPS = 1e-6
NEG_INF = -1e30

LANES = 128
HALF_D = D_MODEL // 2
SEG_W = 512
SEG_AQ, SEG_AK, SEG_AV, SEG_BQ = 0, 1, 2, 3
SEG_C0 = 4
SEG_DQ, SEG_DK, SEG_DV = 13, 14, 15
N_SEG = 16
QKV_W = N_SEG * SEG_W
BKV_LO, BKV_HI = 2048, 2304

VMEM_LIMIT = 56 * 1024 * 1024

MOE_ROWS = 1024
MOE_TF = 256
GATHER_ROWS = 512
COMBINE_ROWS = 256


def _params(*sem):
    return pltpu.CompilerParams(dimension_semantics=sem, vmem_limit_bytes=VMEM_LIMIT)


def _alibi_slopes(n):
    return (2.0 ** (-8.0 * np.arange(1, n + 1, dtype=np.float64) / n)).astype(np.float32)


def _group_mean_sq(x):
    r = lax.broadcasted_iota(I32, (LANES, LANES), 0) // HEAD_DIM
    c = lax.broadcasted_iota(I32, (LANES, LANES), 1) // HEAD_DIM
    ones = jnp.where(r == c, 1.0, 0.0).astype(BF16)
    return jnp.dot((x * x).astype(BF16), ones, preferred_element_type=F32) * (1.0 / HEAD_DIM)


def _head_norm(x, gain):
    return x * lax.rsqrt(_group_mean_sq(x) + RMS_EPS) * gain


def _lo_lanes(shape):
    return lax.broadcasted_iota(I32, shape, 1) < HEAD_DIM


def _split_heads(x):
    lo = _lo_lanes(x.shape)
    return jnp.concatenate([jnp.where(lo, x, 0.0), jnp.where(lo, 0.0, x)], axis=0).astype(BF16)


def _qk(q, k):
    return lax.dot_general(q, k, (((1,), (1,)), ((), ())), preferred_element_type=F32)


def _pack_pair(a, b):
    ua = lax.bitcast_convert_type(a.astype(BF16).astype(F32), U32)
    ub = lax.bitcast_convert_type(b.astype(BF16).astype(F32), U32)
    return (ua >> 16) | (ub & jnp.uint32(0xFFFF0000))


def _unpack_pair(w):
    a = lax.bitcast_convert_type(w << 16, F32)
    b = lax.bitcast_convert_type(w & jnp.uint32(0xFFFF0000), F32)
    return a, b


def _ada_kernel(c_ref, w_ref, b_ref, o_ref):
    c = c_ref[...]
    h = (c * jax.nn.sigmoid(c)).astype(BF16)
    o_ref[...] = jnp.dot(h, w_ref[...].astype(BF16), preferred_element_type=F32) + b_ref[...]


def _ada_mod(c, w_ada, b_ada):
    nb, d = c.shape
    depth, _, n = w_ada.shape
    tn = 1024
    return pl.pallas_call(
        _ada_kernel,
        grid=(depth, n // tn),
        in_specs=[pl.BlockSpec((nb, d), lambda l, j: (0, 0)),
                  pl.BlockSpec((None, d, tn), lambda l, j: (l, 0, j)),
                  pl.BlockSpec((None, 1, tn), lambda l, j: (l, 0, j))],
        out_specs=pl.BlockSpec((None, nb, tn), lambda l, j: (l, 0, j)),
        out_shape=jax.ShapeDtypeStruct((depth, nb, n), F32),
        compiler_params=_params("parallel", "parallel"),
        name="ada_mod",
    )(c, w_ada, b_ada.reshape(depth, 1, n))


def _modulated_norm(x, g_ref, sc_ref, sh_ref):
    ms = jnp.mean(x * x, axis=-1, keepdims=True)
    y = x * lax.rsqrt(ms + RMS_EPS) * g_ref[...]
    return y * (1.0 + sc_ref[...]) + sh_ref[...]


def _normmod_kernel(x_ref, g_ref, sc_ref, sh_ref, o_ref):
    o_ref[...] = _modulated_norm(x_ref[...], g_ref, sc_ref, sh_ref).astype(o_ref.dtype)


def _normmod(x, g, sc, sh):
    nb, s, d = x.shape
    tm = 512
    return pl.pallas_call(
        _normmod_kernel,
        grid=(nb, s // tm),
        in_specs=[pl.BlockSpec((None, tm, d), lambda b, i: (b, i, 0)),
                  pl.BlockSpec((1, d), lambda b, i: (0, 0)),
                  pl.BlockSpec((None, 1, d), lambda b, i: (b, 0, 0)),
                  pl.BlockSpec((None, 1, d), lambda b, i: (b, 0, 0))],
        out_specs=pl.BlockSpec((None, tm, d), lambda b, i: (b, i, 0)),
        out_shape=jax.ShapeDtypeStruct((nb, s, d), BF16),
        compiler_params=_params("parallel", "parallel"),
        name="normmod",
    )(x, g.reshape(1, d), sc, sh)


def _mm_kernel(x_ref, w_ref, o_ref):
    o_ref[...] = jnp.dot(x_ref[...], w_ref[...], preferred_element_type=F32).astype(o_ref.dtype)


def _mm_sigmoid_kernel(x_ref, w_ref, b_ref, o_ref):
    acc = jnp.dot(x_ref[...], w_ref[...], preferred_element_type=F32) + b_ref[...]
    o_ref[...] = jax.nn.sigmoid(acc).astype(o_ref.dtype)


def _mm_resid_kernel(x_ref, w_ref, r_ref, g_ref, o_ref):
    acc = jnp.dot(x_ref[...], w_ref[...], preferred_element_type=F32)
    o_ref[...] = r_ref[...] + g_ref[...] * acc


def _mm_tiles(m, n):
    tm = min(1024, m)
    tn = min(512, n)
    assert m % tm == 0 and n % tn == 0
    return tm, tn


def _matmul(x, w, name):
    m, k = x.shape
    n = w.shape[1]
    tm, tn = _mm_tiles(m, n)
    return pl.pallas_call(
        _mm_kernel,
        grid=(m // tm, n // tn),
        in_specs=[pl.BlockSpec((tm, k), lambda i, j: (i, 0)),
                  pl.BlockSpec((k, tn), lambda i, j: (0, j))],
        out_specs=pl.BlockSpec((tm, tn), lambda i, j: (i, j)),
        out_shape=jax.ShapeDtypeStruct((m, n), BF16),
        compiler_params=_params("parallel", "arbitrary"),
        name=name,
    )(x, w)


def _matmul_sigmoid(x, w, b, name):
    m, k = x.shape
    n = w.shape[1]
    tm, tn = _mm_tiles(m, n)
    return pl.pallas_call(
        _mm_sigmoid_kernel,
        grid=(m // tm, n // tn),
        in_specs=[pl.BlockSpec((tm, k), lambda i, j: (i, 0)),
                  pl.BlockSpec((k, tn), lambda i, j: (0, j)),
                  pl.BlockSpec((1, tn), lambda i, j: (0, j))],
        out_specs=pl.BlockSpec((tm, tn), lambda i, j: (i, j)),
        out_shape=jax.ShapeDtypeStruct((m, n), BF16),
        compiler_params=_params("parallel", "arbitrary"),
        name=name,
    )(x, w, b.reshape(1, n))


def _matmul_residual(x, w, resid, gate, seq, name):
    m, k = x.shape
    n = w.shape[1]
    tm, tn = _mm_tiles(m, n)
    assert seq % tm == 0
    per = seq // tm
    return pl.pallas_call(
        _mm_resid_kernel,
        grid=(m // tm, n // tn),
        in_specs=[pl.BlockSpec((tm, k), lambda i, j: (i, 0)),
                  pl.BlockSpec((k, tn), lambda i, j: (0, j)),
                  pl.BlockSpec((tm, tn), lambda i, j: (i, j)),
                  pl.BlockSpec((None, 1, tn), lambda i, j: (i // per, 0, j))],
        out_specs=pl.BlockSpec((tm, tn), lambda i, j: (i, j)),
        out_shape=jax.ShapeDtypeStruct((m, n), F32),
        compiler_params=_params("parallel", "arbitrary"),
        name=name,
    )(x, w, resid, gate)


def _attn_a_kernel(q_ref, k_ref, v_ref, gq_ref, gk_ref, lam_ref, sub_ref, slope_ref, o_ref, kn_ref,
                   *, lam_init, tq, seq):
    qi = pl.program_id(2)

    @pl.when(qi == 0)
    def _():
        kn_ref[...] = _head_norm(k_ref[...].astype(F32), gk_ref[...]).astype(BF16)

    lv = lam_ref[...]
    s01 = jnp.sum(lv[0:1] * lv[1:2], axis=-1, keepdims=True)
    s23 = jnp.sum(lv[2:3] * lv[3:4], axis=-1, keepdims=True)
    lam = jnp.exp(s01) - jnp.exp(s23) + lam_init

    qn = _head_norm(q_ref[...].astype(F32), gq_ref[...])
    q2 = _split_heads(qn)
    sc = _qk(q2, kn_ref[...])
    row = qi * tq + lax.broadcasted_iota(I32, (tq, seq), 0)
    col = lax.broadcasted_iota(I32, (tq, seq), 1)
    bias = slope_ref[...] * jnp.abs(row - col).astype(F32)
    v = v_ref[...]

    def component(s):
        s = s - bias
        m = jnp.max(s, axis=-1, keepdims=True)
        p = jnp.exp(s - m)
        l = jnp.sum(p, axis=-1, keepdims=True)
        return jnp.dot(p.astype(BF16), v, preferred_element_type=F32) / l

    o = component(sc[:tq]) - lam * component(sc[tq:])
    ms = jnp.mean(o * o, axis=-1, keepdims=True)
    o_ref[...] = (o * lax.rsqrt(ms + RMS_EPS) * sub_ref[...] * (1.0 - lam_init)).astype(o_ref.dtype)


def _attn_a(qkv, gq, gk, a_lambda, a_subln, lam_init):
    nb, seq, _ = qkv.shape
    tq = 256
    per = SEG_W // LANES
    slopes = jnp.asarray(_alibi_slopes(A_HEADS)).reshape(A_HEADS, 1, 1)
    kern = functools.partial(_attn_a_kernel, lam_init=lam_init, tq=tq, seq=seq)
    return pl.pallas_call(
        kern,
        grid=(nb, A_HEADS, seq // tq),
        in_specs=[pl.BlockSpec((None, tq, LANES), lambda b, h, i: (b, i, SEG_AQ * per + h)),
                  pl.BlockSpec((None, seq, LANES), lambda b, h, i: (b, 0, SEG_AK * per + h)),
                  pl.BlockSpec((None, seq, LANES), lambda b, h, i: (b, 0, SEG_AV * per + h)),
                  pl.BlockSpec((1, LANES), lambda b, h, i: (0, 0)),
                  pl.BlockSpec((1, LANES), lambda b, h, i: (0, 0)),
                  pl.BlockSpec((4, HEAD_DIM), lambda b, h, i: (0, 0)),
                  pl.BlockSpec((1, LANES), lambda b, h, i: (0, 0)),
                  pl.BlockSpec((None, 1, 1), lambda b, h, i: (h, 0, 0))],
        out_specs=pl.BlockSpec((None, tq, LANES), lambda b, h, i: (b, i, h)),
        out_shape=jax.ShapeDtypeStruct((nb, seq, BRANCH_W), BF16),
        scratch_shapes=[pltpu.VMEM((seq, LANES), BF16)],
        compiler_params=_params("parallel", "arbitrary", "arbitrary"),
        name="attn_a",
    )(qkv, qkv, qkv, gq, gk, a_lambda, a_subln.reshape(1, LANES), slopes)


def _attn_b_kernel(q_ref, k_ref, v_ref, gq_ref, gk_ref, slope_ref, sink_ref, o_ref, k2_ref, v2_ref,
                   *, tq, nk, seq):
    j = pl.program_id(1)
    qi = pl.program_id(2)

    @pl.when(qi == 0)
    def _():
        half = (lax.broadcasted_iota(I32, (seq, LANES), 1) >= HEAD_DIM).astype(I32)
        own = half == j
        kn = _head_norm(k_ref[...].astype(F32), gk_ref[...])
        k2_ref[...] = jnp.where(own, kn, pltpu.roll(kn, HEAD_DIM, axis=1)).astype(BF16)
        v = v_ref[...].astype(F32)
        v2_ref[...] = jnp.where(own, v, pltpu.roll(v, HEAD_DIM, axis=1)).astype(BF16)

    ws = pl.multiple_of(jnp.clip(qi * tq - B_WINDOW, 0, seq - nk), LANES)
    kw = k2_ref[pl.ds(ws, nk), :]
    vw = v2_ref[pl.ds(ws, nk), :]
    q = q_ref[...].astype(F32)
    g = gq_ref[...]
    qa = _head_norm(q[:, :LANES], g)
    qb = _head_norm(q[:, LANES:], g)
    q4 = jnp.concatenate([_split_heads(qa), _split_heads(qb)], axis=0)
    sc = _qk(q4, kw)
    row = qi * tq + lax.broadcasted_iota(I32, (tq, nk), 0)
    col = ws + lax.broadcasted_iota(I32, (tq, nk), 1)
    rel = jnp.abs(row - col)
    valid = rel <= B_WINDOW
    relf = rel.astype(F32)
    outs = []
    for gi in range(B_HEADS // B_KV_HEADS):
        s = jnp.where(valid, sc[gi * tq:(gi + 1) * tq] - slope_ref[gi] * relf, NEG_INF)
        m = jnp.max(s, axis=-1, keepdims=True)
        p = jnp.exp(s - m)
        l = jnp.sum(p, axis=-1, keepdims=True) + jnp.exp(sink_ref[gi] - m)
        outs.append(jnp.dot(p.astype(BF16), vw, preferred_element_type=F32) / l)
    lo = _lo_lanes((tq, LANES))
    o = jnp.concatenate([jnp.where(lo, outs[0], outs[1]), jnp.where(lo, outs[2], outs[3])], axis=1)
    o_ref[...] = o.astype(o_ref.dtype)


def _attn_b(qkv, kvb, gq, gk, b_sink):
    nb, seq, _ = qkv.shape
    tq = 128
    nk = tq + 2 * B_WINDOW
    grp = B_HEADS // B_KV_HEADS
    qw = grp * HEAD_DIM
    slopes = jnp.asarray(_alibi_slopes(B_HEADS)).reshape(B_HEADS, 1, 1)
    kern = functools.partial(_attn_b_kernel, tq=tq, nk=nk, seq=seq)
    return pl.pallas_call(
        kern,
        grid=(nb, B_KV_HEADS, seq // tq),
        in_specs=[pl.BlockSpec((None, tq, qw), lambda b, j, i: (b, i, SEG_BQ * (SEG_W // qw) + j)),
                  pl.BlockSpec((None, seq, LANES), lambda b, j, i: (b, 0, 0)),
                  pl.BlockSpec((None, seq, LANES), lambda b, j, i: (b, 0, 1)),
                  pl.BlockSpec((1, LANES), lambda b, j, i: (0, 0)),
                  pl.BlockSpec((1, LANES), lambda b, j, i: (0, 0)),
                  pl.BlockSpec((grp, 1, 1), lambda b, j, i: (j, 0, 0)),
                  pl.BlockSpec((grp, 1, 1), lambda b, j, i: (j, 0, 0))],
        out_specs=pl.BlockSpec((None, tq, qw), lambda b, j, i: (b, i, j)),
        out_shape=jax.ShapeDtypeStruct((nb, seq, BRANCH_W), BF16),
        scratch_shapes=[pltpu.VMEM((seq, LANES), BF16), pltpu.VMEM((seq, LANES), BF16)],
        compiler_params=_params("parallel", "arbitrary", "arbitrary"),
        name="attn_b",
    )(qkv, kvb, kvb, gq, gk, slopes, b_sink.reshape(B_HEADS, 1, 1))


def _attn_c_kernel(q_ref, k_ref, v_ref, gq_ref, gk_ref, slope_ref, o_ref, lse_ref, kn_ref,
                   *, tq, nk, sub, side, dil):
    qi = pl.program_id(2)
    n_pairs = SEG_W // LANES

    @pl.when(qi == 0)
    def _():
        for hp in range(n_pairs):
            cs = slice(hp * LANES, (hp + 1) * LANES)
            kn_ref[:, cs] = _head_norm(k_ref[:, cs].astype(F32), gk_ref[...]).astype(BF16)

    ws = pl.multiple_of(jnp.clip(qi * tq - side, 0, sub - nk), HEAD_DIM)
    row = qi * tq + lax.broadcasted_iota(I32, (tq, nk), 0)
    col = ws + lax.broadcasted_iota(I32, (tq, nk), 1)
    rel = jnp.abs(row - col)
    valid = rel <= side
    relf = (dil * rel).astype(F32)
    lo = _lo_lanes((tq, LANES))
    for hp in range(n_pairs):
        cs = slice(hp * LANES, (hp + 1) * LANES)
        qn = _head_norm(q_ref[:, cs].astype(F32), gq_ref[...])
        sc = _qk(_split_heads(qn), kn_ref[pl.ds(ws, nk), cs])
        vw = v_ref[pl.ds(ws, nk), cs]
        o2, l2 = [], []
        for t in range(2):
            s = jnp.where(valid, sc[t * tq:(t + 1) * tq] - slope_ref[2 * hp + t] * relf, NEG_INF)
            m = jnp.max(s, axis=-1, keepdims=True)
            p = jnp.exp(s - m)
            l = jnp.sum(p, axis=-1, keepdims=True)
            o2.append(jnp.dot(p.astype(BF16), vw, preferred_element_type=F32) / l)
            l2.append(m + jnp.log(l))
        o_ref[:, cs] = jnp.where(lo, o2[0], o2[1]).astype(o_ref.dtype)
        lse_ref[:, cs] = jnp.where(lo, l2[0], l2[1])


def _attn_c_group(qkv, gi, gq, gk):
    nb, seq, _ = qkv.shape
    win, dil = C_DILATIONS[gi]
    side = win // (2 * dil)
    sub = seq // dil
    tq = min(256, sub)
    nk = min(tq + 2 * side, sub)
    seg = SEG_C0 + 3 * gi
    slopes = jnp.asarray(_alibi_slopes(N_DIL * C_HEADS).reshape(N_DIL, C_HEADS)[gi]).reshape(C_HEADS, 1, 1)
    qv = qkv.reshape(nb, sub, dil * QKV_W)
    kern = functools.partial(_attn_c_kernel, tq=tq, nk=nk, sub=sub, side=side, dil=dil)
    o, lse = pl.pallas_call(
        kern,
        grid=(nb, dil, sub // tq),
        in_specs=[pl.BlockSpec((None, tq, SEG_W), lambda b, r, i: (b, i, r * N_SEG + seg)),
                  pl.BlockSpec((None, sub, SEG_W), lambda b, r, i: (b, 0, r * N_SEG + seg + 1)),
                  pl.BlockSpec((None, sub, SEG_W), lambda b, r, i: (b, 0, r * N_SEG + seg + 2)),
                  pl.BlockSpec((1, LANES), lambda b, r, i: (0, 0)),
                  pl.BlockSpec((1, LANES), lambda b, r, i: (0, 0)),
                  pl.BlockSpec((C_HEADS, 1, 1), lambda b, r, i: (0, 0, 0))],
        out_specs=[pl.BlockSpec((None, tq, SEG_W), lambda b, r, i: (b, i, r)),
                   pl.BlockSpec((None, tq, SEG_W), lambda b, r, i: (b, i, r))],
        out_shape=[jax.ShapeDtypeStruct((nb, sub, dil * SEG_W), BF16),
                   jax.ShapeDtypeStruct((nb, sub, dil * SEG_W), F32)],
        scratch_shapes=[pltpu.VMEM((sub, SEG_W), BF16)],
        compiler_params=_params("parallel", "arbitrary", "arbitrary"),
        name=f"attn_c{gi}",
    )(qv, qv, qv, gq, gk, slopes)
    return o.reshape(nb, seq, SEG_W), lse.reshape(nb, seq, SEG_W)


def _attn_d_kernel(q_ref, k_ref, v_ref, gq_ref, gk_ref, bias_ref, o_ref, kn_ref, *, rows, nk):
    r = pl.program_id(1)
    n_pairs = SEG_W // LANES

    @pl.when(r == 0)
    def _():
        for hp in range(n_pairs):
            cs = slice(hp * LANES, (hp + 1) * LANES)
            kn_ref[:, cs] = _head_norm(k_ref[:, cs].astype(F32), gk_ref[...]).astype(BF16)

    kr = nk // GRID_W
    ws = pl.multiple_of(jnp.clip(r - kr // 2, 0, rows - kr) * GRID_W, GRID_W)
    lo = _lo_lanes((GRID_W, LANES))
    for hp in range(n_pairs):
        cs = slice(hp * LANES, (hp + 1) * LANES)
        qn = _head_norm(q_ref[:, cs].astype(F32), gq_ref[...])
        sc = _qk(_split_heads(qn), kn_ref[pl.ds(ws, nk), cs])
        vw = v_ref[pl.ds(ws, nk), cs]
        o2 = []
        for t in range(2):
            s = sc[t * GRID_W:(t + 1) * GRID_W] + bias_ref[2 * hp + t]
            m = jnp.max(s, axis=-1, keepdims=True)
            p = jnp.exp(s - m)
            l = jnp.sum(p, axis=-1, keepdims=True)
            o2.append(jnp.dot(p.astype(BF16), vw, preferred_element_type=F32) / l)
        o_ref[:, cs] = jnp.where(lo, o2[0], o2[1]).astype(o_ref.dtype)


def _na_bias_table(rpb, rows):
    kr = min(NA_ROWS, rows)
    qc = np.arange(GRID_W)[:, None]
    kc = np.arange(GRID_W)[None, :]
    col_start = np.clip(qc - NA_COLS // 2, 0, GRID_W - NA_COLS)
    ok = (kc >= col_start) & (kc < col_start + NA_COLS)
    dc = np.clip(kc - qc + NA_COLS - 1, 0, 2 * NA_COLS - 2)
    drow = np.arange(kr)[:, None] + np.arange(kr)[None, :]
    t = rpb[:, drow][:, :, :, dc]
    t = jnp.where(ok[None, None, None], t.astype(F32), NEG_INF)
    t = t.transpose(1, 0, 3, 2, 4)
    return t.reshape(kr, D_HEADS, GRID_W, kr * GRID_W)


def _attn_d(qkv, gq, gk, d_rpb):
    nb, seq, _ = qkv.shape
    rows = seq // GRID_W
    kr = min(NA_ROWS, rows)
    nk = kr * GRID_W
    table = _na_bias_table(d_rpb, rows)

    def d0(r):
        return jnp.clip(r - kr // 2, 0, rows - kr) - r + NA_ROWS - 1

    kern = functools.partial(_attn_d_kernel, rows=rows, nk=nk)
    return pl.pallas_call(
        kern,
        grid=(nb, rows),
        in_specs=[pl.BlockSpec((None, GRID_W, SEG_W), lambda b, r: (b, r, SEG_DQ)),
                  pl.BlockSpec((None, seq, SEG_W), lambda b, r: (b, 0, SEG_DK)),
                  pl.BlockSpec((None, seq, SEG_W), lambda b, r: (b, 0, SEG_DV)),
                  pl.BlockSpec((1, LANES), lambda b, r: (0, 0)),
                  pl.BlockSpec((1, LANES), lambda b, r: (0, 0)),
                  pl.BlockSpec((None, D_HEADS, GRID_W, nk), lambda b, r: (d0(r), 0, 0, 0))],
        out_specs=pl.BlockSpec((None, GRID_W, SEG_W), lambda b, r: (b, r, 0)),
        out_shape=jax.ShapeDtypeStruct((nb, seq, BRANCH_W), BF16),
        scratch_shapes=[pltpu.VMEM((seq, SEG_W), BF16)],
        compiler_params=_params("parallel", "arbitrary"),
        name="attn_d",
    )(qkv, qkv, qkv, gq, gk, table)


def _merge_kernel(oa_ref, ob_ref, oc0_ref, oc1_ref, oc2_ref, l0_ref, l1_ref, l2_ref, od_ref,
                  gates_ref, wb_ref, o_ref):
    l0, l1, l2 = l0_ref[...], l1_ref[...], l2_ref[...]
    lm = jnp.maximum(jnp.maximum(l0, l1), l2)
    e0, e1, e2 = jnp.exp(l0 - lm), jnp.exp(l1 - lm), jnp.exp(l2 - lm)
    oc = (e0 * oc0_ref[...].astype(F32) + e1 * oc1_ref[...].astype(F32)
          + e2 * oc2_ref[...].astype(F32)) / (e0 + e1 + e2)
    branches = (oa_ref[...], ob_ref[...], oc.astype(BF16), od_ref[...])
    acc = None
    for n, br in enumerate(branches):
        proj = jnp.dot(br, wb_ref[n], preferred_element_type=F32)
        term = gates_ref[:, n * D_MODEL:(n + 1) * D_MODEL].astype(F32) * proj
        acc = term if acc is None else acc + term
    o_ref[...] = acc.astype(o_ref.dtype)


def _merge(oa, ob, oc, lses, od, gates, w_branch):
    t = oa.shape[0]
    tm = 256
    blk = lambda w: pl.BlockSpec((tm, w), lambda i: (i, 0))
    return pl.pallas_call(
        _merge_kernel,
        grid=(t // tm,),
        in_specs=[blk(BRANCH_W)] * 9 + [blk(N_BRANCH * D_MODEL),
                                         pl.BlockSpec((N_BRANCH, BRANCH_W, D_MODEL), lambda i: (0, 0, 0))],
        out_specs=blk(D_MODEL),
        out_shape=jax.ShapeDtypeStruct((t, D_MODEL), BF16),
        compiler_params=_params("parallel"),
        name="merge",
    )(oa, ob, oc[0], oc[1], oc[2], lses[0], lses[1], lses[2], od, gates, w_branch)


def _route_kernel(x_ref, g_ref, sc_ref, sh_ref, wr_ref, br_ref, hf_ref, idx_ref, p_ref):
    h = _modulated_norm(x_ref[...], g_ref, sc_ref, sh_ref)
    hf_ref[...] = _pack_pair(h[:, :HALF_D], h[:, HALF_D:])
    logits = jnp.dot(h.astype(BF16), wr_ref[...], preferred_element_type=F32) + br_ref[...]
    lane = lax.broadcasted_iota(I32, logits.shape, 1)
    idx_t = jnp.zeros(logits.shape, I32)
    e_t = jnp.zeros(logits.shape, F32)
    den = None
    top = None
    for k in range(TOP_K):
        m = jnp.max(logits, axis=-1, keepdims=True)
        idx = jnp.min(jnp.where(logits == m, lane, LANES), axis=-1, keepdims=True)
        logits = jnp.where(lane == idx, -jnp.inf, logits)
        if k == 0:
            top = m
        e = jnp.exp(m - top)
        den = e if den is None else den + e
        idx_t = jnp.where(lane == k, idx, idx_t)
        e_t = jnp.where(lane == k, e, e_t)
    idx_ref[...] = idx_t
    p_ref[...] = e_t / den


def _route(x, g, sc, sh, w_router, b_router):
    nb, s, d = x.shape
    tm = 256
    wr = jnp.zeros((d, LANES), BF16).at[:, :N_EXPERTS].set(w_router.astype(BF16))
    br = jnp.full((1, LANES), NEG_INF, F32).at[0, :N_EXPERTS].set(b_router)
    per = s // tm
    return pl.pallas_call(
        _route_kernel,
        grid=(nb, per),
        in_specs=[pl.BlockSpec((None, tm, d), lambda b, i: (b, i, 0)),
                  pl.BlockSpec((1, d), lambda b, i: (0, 0)),
                  pl.BlockSpec((None, 1, d), lambda b, i: (b, 0, 0)),
                  pl.BlockSpec((None, 1, d), lambda b, i: (b, 0, 0)),
                  pl.BlockSpec((d, LANES), lambda b, i: (0, 0)),
                  pl.BlockSpec((1, LANES), lambda b, i: (0, 0))],
        out_specs=[pl.BlockSpec((tm, HALF_D), lambda b, i: (b * per + i, 0)),
                   pl.BlockSpec((tm, LANES), lambda b, i: (b * per + i, 0)),
                   pl.BlockSpec((tm, LANES), lambda b, i: (b * per + i, 0))],
        out_shape=[jax.ShapeDtypeStruct((nb * s, HALF_D), U32),
                   jax.ShapeDtypeStruct((nb * s, LANES), I32),
                   jax.ShapeDtypeStruct((nb * s, LANES), F32)],
        compiler_params=_params("parallel", "parallel"),
        name="route",
    )(x, g.reshape(1, d), sc, sh, wr, br)


def _gather_kernel(nrows_ref, tok_ref, src_ref, o_ref, sem, *, rows):
    i = pl.program_id(0)

    def row_copy(k, t):
        return pltpu.make_async_copy(src_ref.at[pl.ds(t, 1), :], o_ref.at[pl.ds(k, 1), :], sem)

    @pl.when(i * rows < nrows_ref[0])
    def _():
        def issue(k, c):
            row_copy(k, tok_ref[0, 0, k]).start()
            return c
        lax.fori_loop(0, rows, issue, 0)

        def drain(k, c):
            row_copy(k, 0).wait()
            return c
        lax.fori_loop(0, rows, drain, 0)

    @pl.when(i * rows >= nrows_ref[0])
    def _():
        o_ref[...] = jnp.zeros(o_ref.shape, o_ref.dtype)


def _gather_rows(src, slot_tok, n_rows_used):
    n_slots = slot_tok.shape[0]
    rows = GATHER_ROWS
    nblk = n_slots // rows
    kern = functools.partial(_gather_kernel, rows=rows)
    return pl.pallas_call(
        kern,
        grid_spec=pltpu.PrefetchScalarGridSpec(
            num_scalar_prefetch=1,
            grid=(nblk,),
            in_specs=[pl.BlockSpec((1, 1, rows), lambda i, n: (i, 0, 0), memory_space=pltpu.SMEM),
                      pl.BlockSpec(memory_space=pl.ANY)],
            out_specs=pl.BlockSpec((rows, HALF_D), lambda i, n: (i, 0)),
            scratch_shapes=[pltpu.SemaphoreType.DMA(())],
        ),
        out_shape=jax.ShapeDtypeStruct((n_slots, HALF_D), U32),
        compiler_params=_params("arbitrary"),
        name="moe_gather",
    )(n_rows_used, slot_tok.reshape(nblk, 1, rows), src)


def _expert_kernel(be_ref, nu_ref, xs_ref, wg_ref, wu_ref, wd_ref, bg_ref, bu_ref, bd_ref, o_ref,
                   xb_ref, acc_ref, *, n_f):
    i = pl.program_id(0)
    f = pl.program_id(1)
    used = i < nu_ref[0]

    @pl.when(used & (f == 0))
    def _():
        lo, hi = _unpack_pair(xs_ref[...])
        xb_ref[:, :HALF_D] = lo.astype(BF16)
        xb_ref[:, HALF_D:] = hi.astype(BF16)

    @pl.when(used)
    def _():
        x = xb_ref[...]
        g = jnp.dot(x, wg_ref[...].astype(BF16), preferred_element_type=F32) + bg_ref[...]
        u = jnp.dot(x, wu_ref[...].astype(BF16), preferred_element_type=F32) + bu_ref[...]
        g = jnp.minimum(g, SWIGLU_LIMIT)
        u = jnp.clip(u, -SWIGLU_LIMIT, SWIGLU_LIMIT)
        a = (u + 1.0) * g * jax.nn.sigmoid(SWIGLU_ALPHA * g)
        part = jnp.dot(a.astype(BF16), wd_ref[...].astype(BF16), preferred_element_type=F32)

        @pl.when(f == 0)
        def _():
            acc_ref[...] = part + bd_ref[...]

        @pl.when(f > 0)
        def _():
            acc_ref[...] += part

    @pl.when(used & (f == n_f - 1))
    def _():
        o_ref[...] = _pack_pair(acc_ref[:, :HALF_D], acc_ref[:, HALF_D:])

    @pl.when(jnp.logical_not(used) & (f == n_f - 1))
    def _():
        o_ref[...] = jnp.zeros(o_ref.shape, o_ref.dtype)


def _experts(xs, block_expert, n_used, w_gate_up, b_gate_up, w_down, b_down):
    n_slots = xs.shape[0]
    rows, tf = MOE_ROWS, MOE_TF
    nblk = n_slots // rows
    n_f = D_FF // tf
    e = w_gate_up.shape[0]
    kern = functools.partial(_expert_kernel, n_f=n_f)

    def fstep(i, f, nu):
        return jnp.where(i < nu[0], f, n_f - 1)

    return pl.pallas_call(
        kern,
        grid_spec=pltpu.PrefetchScalarGridSpec(
            num_scalar_prefetch=2,
            grid=(nblk, n_f),
            in_specs=[pl.BlockSpec((rows, HALF_D), lambda i, f, be, nu: (i, 0)),
                      pl.BlockSpec((None, D_MODEL, tf), lambda i, f, be, nu: (be[i], 0, fstep(i, f, nu))),
                      pl.BlockSpec((None, D_MODEL, tf), lambda i, f, be, nu: (be[i], 0, n_f + fstep(i, f, nu))),
                      pl.BlockSpec((None, tf, D_MODEL), lambda i, f, be, nu: (be[i], fstep(i, f, nu), 0)),
                      pl.BlockSpec((None, 1, tf), lambda i, f, be, nu: (be[i], 0, fstep(i, f, nu))),
                      pl.BlockSpec((None, 1, tf), lambda i, f, be, nu: (be[i], 0, n_f + fstep(i, f, nu))),
                      pl.BlockSpec((None, 1, D_MODEL), lambda i, f, be, nu: (be[i], 0, 0))],
            out_specs=pl.BlockSpec((rows, HALF_D), lambda i, f, be, nu: (i, 0)),
            scratch_shapes=[pltpu.VMEM((rows, D_MODEL), BF16), pltpu.VMEM((rows, D_MODEL), F32)],
        ),
        out_shape=jax.ShapeDtypeStruct((n_slots, HALF_D), U32),
        compiler_params=_params("arbitrary", "arbitrary"),
        name="moe_experts",
    )(block_expert, n_used, xs, w_gate_up, w_gate_up, w_down,
      b_gate_up.reshape(e, 1, 2 * D_FF), b_gate_up.reshape(e, 1, 2 * D_FF), b_down.reshape(e, 1, D_MODEL))


def _combine_kernel(dest_ref, src_ref, p_ref, x_ref, g_ref, o_ref, buf_ref, sem, *, rows):
    def row_copy(k, t, d):
        return pltpu.make_async_copy(src_ref.at[pl.ds(d, 1), :], buf_ref.at[k, pl.ds(t, 1), :], sem)

    for k in range(TOP_K):
        def issue(t, c, k=k):
            row_copy(k, t, dest_ref[0, 0, t * TOP_K + k]).start()
            return c
        lax.fori_loop(0, rows, issue, 0)
    for k in range(TOP_K):
        def drain(t, c, k=k):
            row_copy(k, t, 0).wait()
            return c
        lax.fori_loop(0, rows, drain, 0)

    p = p_ref[...]
    y_lo = None
    y_hi = None
    for k in range(TOP_K):
        lo, hi = _unpack_pair(buf_ref[k])
        pk = p[:, k:k + 1]
        y_lo = pk * lo if y_lo is None else y_lo + pk * lo
        y_hi = pk * hi if y_hi is None else y_hi + pk * hi
    o_ref[:, :HALF_D] = x_ref[:, :HALF_D] + g_ref[:, :HALF_D] * y_lo
    o_ref[:, HALF_D:] = x_ref[:, HALF_D:] + g_ref[:, HALF_D:] * y_hi


def _combine(outs, dest, probs, x, gate, seq):
    t, d = x.shape
    rows = COMBINE_ROWS
    nblk = t // rows
    per = seq // rows
    kern = functools.partial(_combine_kernel, rows=rows)
    return pl.pallas_call(
        kern,
        grid=(nblk,),
        in_specs=[pl.BlockSpec((1, 1, rows * TOP_K), lambda i: (i, 0, 0), memory_space=pltpu.SMEM),
                  pl.BlockSpec(memory_space=pl.ANY),
                  pl.BlockSpec((rows, LANES), lambda i: (i, 0)),
                  pl.BlockSpec((rows, d), lambda i: (i, 0)),
                  pl.BlockSpec((None, 1, d), lambda i: (i // per, 0, 0))],
        out_specs=pl.BlockSpec((rows, d), lambda i: (i, 0)),
        out_shape=jax.ShapeDtypeStruct((t, d), F32),
        scratch_shapes=[pltpu.VMEM((TOP_K, rows, HALF_D), U32), pltpu.SemaphoreType.DMA(())],
        compiler_params=_params("arbitrary"),
        name="moe_combine",
    )(dest.reshape(nblk, 1, rows * TOP_K), outs, probs, x, gate)


def _routing_plan(top_idx, n_tok):
    expert = top_idx.reshape(-1)
    token = jnp.arange(n_tok * TOP_K, dtype=I32) // TOP_K
    onehot = jax.nn.one_hot(expert, N_EXPERTS, dtype=I32)
    csum = jnp.cumsum(onehot, axis=0)
    rank = jnp.take_along_axis(csum, expert[:, None], axis=1)[:, 0] - 1
    padded = (csum[-1] + MOE_ROWS - 1) // MOE_ROWS * MOE_ROWS
    pad_end = jnp.cumsum(padded)
    dest = (pad_end[expert] - padded[expert] + rank).astype(I32)
    n_blocks = -(-(n_tok * TOP_K) // MOE_ROWS) + N_EXPERTS
    n_slots = n_blocks * MOE_ROWS
    slot_tok = jnp.zeros((n_slots,), I32).at[dest].set(token)
    block_expert = jnp.minimum(
        jnp.searchsorted(pad_end, jnp.arange(n_blocks, dtype=I32) * MOE_ROWS, side='right'),
        N_EXPERTS - 1).astype(I32)
    n_rows_used = pad_end[-1:].astype(I32)
    return dest, slot_tok, block_expert, n_rows_used


def _moe(x, g, sc, sh, gate, w_router, b_router, w_gate_up, b_gate_up, w_down, b_down):
    nb, seq, d = x.shape
    n_tok = nb * seq
    hf, idx_t, p_t = _route(x, g, sc, sh, w_router, b_router)
    dest, slot_tok, block_expert, n_rows_used = _routing_plan(idx_t[:, :TOP_K], n_tok)
    xs = _gather_rows(hf, slot_tok, n_rows_used)
    outs = _experts(xs, block_expert, n_rows_used // MOE_ROWS, w_gate_up, b_gate_up, w_down, b_down)
    y = _combine(outs, dest, p_t, x.reshape(n_tok, d), gate, seq)
    return y.reshape(nb, seq, d)


def _pair_gain(g, scale=1.0):
    return (jnp.tile(g.astype(F32), LANES // HEAD_DIM) * scale).reshape(1, LANES)


def _token_mixer(x, layer, lam_init, g, sc, sh, gate, p):
    nb, seq, d = x.shape
    n_tok = nb * seq
    hm = _normmod(x, g, sc, sh).reshape(n_tok, d)
    w_in = p['w_in'][layer]
    w_main = jnp.concatenate([w_in[:, :BKV_LO], w_in[:, BKV_HI:]], axis=1).astype(BF16)
    qkv = _matmul(hm, w_main, "in_proj").reshape(nb, seq, QKV_W)
    kvb = _matmul(hm, w_in[:, BKV_LO:BKV_HI].astype(BF16), "in_proj_bkv").reshape(nb, seq, BKV_HI - BKV_LO)
    qs = HEAD_DIM ** -0.5
    oa = _attn_a(qkv, _pair_gain(p['a_qk_norm'][layer, 0], qs), _pair_gain(p['a_qk_norm'][layer, 1]),
                 p['a_lambda'][layer], p['a_subln'][layer], lam_init)
    ob = _attn_b(qkv, kvb, _pair_gain(p['b_qk_norm'][layer, 0], qs), _pair_gain(p['b_qk_norm'][layer, 1]),
                 p['b_sink'][layer])
    ocs, lses = [], []
    for gi in range(N_DIL):
        o, lse = _attn_c_group(qkv, gi, _pair_gain(p['c_qk_norm'][layer, gi, 0], qs),
                               _pair_gain(p['c_qk_norm'][layer, gi, 1]))
        ocs.append(o.reshape(n_tok, BRANCH_W))
        lses.append(lse.reshape(n_tok, BRANCH_W))
    od = _attn_d(qkv, _pair_gain(p['d_qk_norm'][layer, 0], qs), _pair_gain(p['d_qk_norm'][layer, 1]),
                 p['d_rpb'][layer])
    gates = _matmul_sigmoid(hm, p['w_gate'][layer].astype(BF16), p['b_gate'][layer], "gate_proj")
    mixed = _merge(oa.reshape(n_tok, BRANCH_W), ob.reshape(n_tok, BRANCH_W), ocs, lses,
                   od.reshape(n_tok, BRANCH_W), gates, p['w_branch'][layer].astype(BF16))
    y = _matmul_residual(mixed, p['w_out'][layer].astype(BF16), x.reshape(n_tok, d), gate, seq, "out_proj")
    return y.reshape(nb, seq, d)


def kernel(x_prompt, x_sample, c_prompt, c_sample, w_ada, b_ada, norm_g, w_in, a_qk_norm, a_lambda, a_subln, b_qk_norm, b_sink, c_qk_norm, d_qk_norm, d_rpb, w_gate, b_gate, w_branch, w_out, w_router, b_router, w_gate_up, b_gate_up, w_down, b_down):
    p = dict(w_in=w_in, a_qk_norm=a_qk_norm, a_lambda=a_lambda, a_subln=a_subln, b_qk_norm=b_qk_norm,
             b_sink=b_sink, c_qk_norm=c_qk_norm, d_qk_norm=d_qk_norm, d_rpb=d_rpb, w_gate=w_gate,
             b_gate=b_gate, w_branch=w_branch, w_out=w_out)
    assert x_prompt.shape[1:] == x_sample.shape[1:]
    n_prompt = x_prompt.shape[0]
    x = jnp.concatenate([x_prompt, x_sample], axis=0)
    c = jnp.concatenate([c_prompt, c_sample], axis=0)
    nb, seq, d = x.shape
    depth = w_ada.shape[0]
    mod = _ada_mod(c, w_ada, b_ada).reshape(depth, nb, 6, 1, d)
    for layer in range(depth):
        sh1, sc1, g1, sh2, sc2, g2 = (mod[layer, :, i] for i in range(6))
        lam_init = 0.8 - 0.6 * math.exp(-0.3 * layer)
        x = _token_mixer(x, layer, lam_init, norm_g[layer, 0], sc1, sh1, g1, p)
        x = _moe(x, norm_g[layer, 1], sc2, sh2, g2, w_router[layer], b_router[layer],
                 w_gate_up[layer], b_gate_up[layer], w_down[layer], b_down[layer])
    return x[:n_prompt], x[n_prompt:]
```

```python
import functools
import math

import numpy as np
import jax
import jax.numpy as jnp
from jax import lax
from jax.experimental import pallas as pl
from jax.experimental.pallas import tpu as pltpu

F32 = jnp.float32
BF16 = jnp.bfloat16
U32 = jnp.uint32
I32 = jnp.int32

D_MODEL = 2048
DEPTH = 4
HEAD_DIM = 64
N_BRANCH = 4
BRANCH_W = D_MODEL // N_BRANCH
A_HEADS = 4
B_HEADS = 8
B_KV_HEADS = 2
B_WINDOW = 128
C_HEADS = 8
C_DILATIONS = ((128, 1), (512, 4), (2048, 16))
N_DIL = 3
GRID_W = 64
D_HEADS = 8
NA_ROWS = 8
NA_COLS = 16
N_EXPERTS = 32
TOP_K = 4
D_FF = D_MODEL
SWIGLU_ALPHA = 1.702
SWIGLU_LIMIT = 7.0
RMS_EPS = 1e-6
NEG_INF = -1e30

LANES = 128
HALF_D = D_MODEL // 2
SEG_W = 512
SEG_AQ, SEG_AK, SEG_AV, SEG_BQ = 0, 1, 2, 3
SEG_C0 = 4
SEG_DQ, SEG_DK, SEG_DV = 7, 8, 9
N_SEG = 10
QKV_W = N_SEG * SEG_W
BKV_LO, BKV_HI = 2048, 2304
C_LO = 2304
C_GROUP_W = 3 * SEG_W
D_LO = C_LO + N_DIL * C_GROUP_W
ROW_TILE = 8

VMEM_LIMIT = 56 * 1024 * 1024

MOE_ROWS = 1024
MOE_TF = 256
MOE_TN = 256
GATHER_ROWS = 512
COMBINE_ROWS = 256
DMA_UNROLL = 8


def _params(*sem):
    return pltpu.CompilerParams(dimension_semantics=sem, vmem_limit_bytes=VMEM_LIMIT)


def _alibi_slopes(n):
    return (2.0 ** (-8.0 * np.arange(1, n + 1, dtype=np.float64) / n)).astype(np.float32)


def _group_mean_sq(x):
    r = lax.broadcasted_iota(I32, (LANES, LANES), 0) // HEAD_DIM
    c = lax.broadcasted_iota(I32, (LANES, LANES), 1) // HEAD_DIM
    ones = jnp.where(r == c, 1.0, 0.0).astype(BF16)
    return jnp.dot((x * x).astype(BF16), ones, preferred_element_type=F32) * (1.0 / HEAD_DIM)


def _head_norm(x, gain):
    return x * lax.rsqrt(_group_mean_sq(x) + RMS_EPS) * gain


def _lo_lanes(shape):
    return lax.broadcasted_iota(I32, shape, 1) < HEAD_DIM


def _split_heads(x):
    lo = _lo_lanes(x.shape)
    return jnp.concatenate([jnp.where(lo, x, 0.0), jnp.where(lo, 0.0, x)], axis=0).astype(BF16)


def _qk(q, k):
    return lax.dot_general(q, k, (((1,), (1,)), ((), ())), preferred_element_type=F32)


def _pack_pair(a, b):
    ua = lax.bitcast_convert_type(a.astype(BF16).astype(F32), U32)
    ub = lax.bitcast_convert_type(b.astype(BF16).astype(F32), U32)
    return (ua >> 16) | (ub & jnp.uint32(0xFFFF0000))


def _unpack_pair(w):
    a = lax.bitcast_convert_type(w << 16, F32)
    b = lax.bitcast_convert_type(w & jnp.uint32(0xFFFF0000), F32)
    return a, b


def _store_row_tiles(ref, val):
    n = val.shape[0]
    for c in range(ROW_TILE):
        lo = val[:, c * LANES:(c + 1) * LANES]
        hi = val[:, HALF_D + c * LANES:HALF_D + (c + 1) * LANES]
        ref[pl.ds(c, n, stride=ROW_TILE), :] = _pack_pair(lo, hi)


def _load_row_tile_chunk(ref, c):
    n = ref.shape[0] // ROW_TILE
    return _unpack_pair(ref[pl.ds(c, n, stride=ROW_TILE), :])


def _ada_kernel(c_ref, w_ref, b_ref, o_ref):
    c = c_ref[...]
    h = (c * jax.nn.sigmoid(c)).astype(BF16)
    o_ref[...] = jnp.dot(h, w_ref[...].astype(BF16), preferred_element_type=F32) + b_ref[...]


def _ada_mod(c, w_ada, b_ada):
    nb, d = c.shape
    depth, _, n = w_ada.shape
    tn = 1024
    return pl.pallas_call(
        _ada_kernel,
        grid=(depth, n // tn),
        in_specs=[pl.BlockSpec((nb, d), lambda l, j: (0, 0)),
                  pl.BlockSpec((None, d, tn), lambda l, j: (l, 0, j)),
                  pl.BlockSpec((None, 1, tn), lambda l, j: (l, 0, j))],
        out_specs=pl.BlockSpec((None, nb, tn), lambda l, j: (l, 0, j)),
        out_shape=jax.ShapeDtypeStruct((depth, nb, n), F32),
        compiler_params=_params("parallel", "parallel"),
        name="ada_mod",
    )(c, w_ada, b_ada.reshape(depth, 1, n))


def _modulated_norm(x, g_ref, sc_ref, sh_ref):
    ms = jnp.mean(x * x, axis=-1, keepdims=True)
    y = x * lax.rsqrt(ms + RMS_EPS) * g_ref[...]
    return y * (1.0 + sc_ref[...]) + sh_ref[...]


def _normmod_kernel(x_ref, g_ref, sc_ref, sh_ref, o_ref):
    o_ref[...] = _modulated_norm(x_ref[...], g_ref, sc_ref, sh_ref).astype(o_ref.dtype)


def _normmod(x, g, sc, sh):
    nb, s, d = x.shape
    tm = 512
    return pl.pallas_call(
        _normmod_kernel,
        grid=(nb, s // tm),
        in_specs=[pl.BlockSpec((None, tm, d), lambda b, i: (b, i, 0)),
                  pl.BlockSpec((1, d), lambda b, i: (0, 0)),
                  pl.BlockSpec((None, 1, d), lambda b, i: (b, 0, 0)),
                  pl.BlockSpec((None, 1, d), lambda b, i: (b, 0, 0))],
        out_specs=pl.BlockSpec((None, tm, d), lambda b, i: (b, i, 0)),
        out_shape=jax.ShapeDtypeStruct((nb, s, d), BF16),
        compiler_params=_params("parallel", "parallel"),
        name="normmod",
    )(x, g.reshape(1, d), sc, sh)


def _mm_kernel(x_ref, w_ref, o_ref):
    o_ref[...] = jnp.dot(x_ref[...], w_ref[...], preferred_element_type=F32).astype(o_ref.dtype)


def _mm_sigmoid_kernel(x_ref, w_ref, b_ref, o_ref):
    acc = jnp.dot(x_ref[...], w_ref[...], preferred_element_type=F32) + b_ref[...]
    o_ref[...] = jax.nn.sigmoid(acc).astype(o_ref.dtype)


def _mm_resid_kernel(x_ref, w_ref, r_ref, g_ref, o_ref):
    acc = jnp.dot(x_ref[...], w_ref[...], preferred_element_type=F32)
    o_ref[...] = r_ref[...] + g_ref[...] * acc


def _mm_tiles(m, n):
    tm = min(1024, m)
    tn = min(512, n)
    assert m % tm == 0 and n % tn == 0
    return tm, tn


def _matmul(x, w, name):
    m, k = x.shape
    n = w.shape[1]
    tm, tn = _mm_tiles(m, n)
    return pl.pallas_call(
        _mm_kernel,
        grid=(m // tm, n // tn),
        in_specs=[pl.BlockSpec((tm, k), lambda i, j: (i, 0)),
                  pl.BlockSpec((k, tn), lambda i, j: (0, j))],
        out_specs=pl.BlockSpec((tm, tn), lambda i, j: (i, j)),
        out_shape=jax.ShapeDtypeStruct((m, n), BF16),
        compiler_params=_params("parallel", "arbitrary"),
        name=name,
    )(x, w)


def _mm_residue_kernel(x_ref, w_ref, o_ref, acc_ref, *, dil):
    tm, tn = acc_ref.shape[1], acc_ref.shape[0] * LANES
    acc = jnp.dot(x_ref[...], w_ref[...], preferred_element_type=F32)
    for c in range(tn // LANES):
        acc_ref[c] = acc[:, c * LANES:(c + 1) * LANES]
    for r in range(dil):
        for c in range(tn // LANES):
            o_ref[r, :, c * LANES:(c + 1) * LANES] = acc_ref[c, pl.ds(r, tm // dil, stride=dil), :].astype(o_ref.dtype)


def _matmul_by_residue(x, w, nb, seq, dil, name):
    m, k = x.shape
    n = w.shape[1]
    tm, tn = _mm_tiles(m, n)
    assert seq % tm == 0 and tm % (dil * 16) == 0
    per = seq // tm
    kern = functools.partial(_mm_residue_kernel, dil=dil)
    return pl.pallas_call(
        kern,
        grid=(m // tm, n // tn),
        in_specs=[pl.BlockSpec((tm, k), lambda i, j: (i, 0)),
                  pl.BlockSpec((k, tn), lambda i, j: (0, j))],
        out_specs=pl.BlockSpec((None, dil, tm // dil, tn), lambda i, j: (i // per, 0, i % per, j)),
        out_shape=jax.ShapeDtypeStruct((nb, dil, seq // dil, n), BF16),
        scratch_shapes=[pltpu.VMEM((tn // LANES, tm, LANES), F32)],
        compiler_params=_params("parallel", "arbitrary"),
        name=name,
    )(x, w)


def _matmul_sigmoid(x, w, b, name):
    m, k = x.shape
    n = w.shape[1]
    tm, tn = _mm_tiles(m, n)
    return pl.pallas_call(
        _mm_sigmoid_kernel,
        grid=(m // tm, n // tn),
        in_specs=[pl.BlockSpec((tm, k), lambda i, j: (i, 0)),
                  pl.BlockSpec((k, tn), lambda i, j: (0, j)),
                  pl.BlockSpec((1, tn), lambda i, j: (0, j))],
        out_specs=pl.BlockSpec((tm, tn), lambda i, j: (i, j)),
        out_shape=jax.ShapeDtypeStruct((m, n), BF16),
        compiler_params=_params("parallel", "arbitrary"),
        name=name,
    )(x, w, b.reshape(1, n))


def _matmul_residual(x, w, resid, gate, seq, name):
    m, k = x.shape
    n = w.shape[1]
    tm, tn = _mm_tiles(m, n)
    assert seq % tm == 0
    per = seq // tm
    return pl.pallas_call(
        _mm_resid_kernel,
        grid=(m // tm, n // tn),
        in_specs=[pl.BlockSpec((tm, k), lambda i, j: (i, 0)),
                  pl.BlockSpec((k, tn), lambda i, j: (0, j)),
                  pl.BlockSpec((tm, tn), lambda i, j: (i, j)),
                  pl.BlockSpec((None, 1, tn), lambda i, j: (i // per, 0, j))],
        out_specs=pl.BlockSpec((tm, tn), lambda i, j: (i, j)),
        out_shape=jax.ShapeDtypeStruct((m, n), F32),
        compiler_params=_params("parallel", "arbitrary"),
        name=name,
    )(x, w, resid, gate)


def _attn_a_kernel(q_ref, k_ref, v_ref, gq_ref, gk_ref, lam_ref, sub_ref, slope_ref, o_ref, kn_ref,
                   *, lam_init, tq, seq):
    qi = pl.program_id(2)

    @pl.when(qi == 0)
    def _():
        kn_ref[...] = _head_norm(k_ref[...].astype(F32), gk_ref[...]).astype(BF16)

    lv = lam_ref[...]
    s01 = jnp.sum(lv[0:1] * lv[1:2], axis=-1, keepdims=True)
    s23 = jnp.sum(lv[2:3] * lv[3:4], axis=-1, keepdims=True)
    lam = jnp.exp(s01) - jnp.exp(s23) + lam_init

    qn = _head_norm(q_ref[...].astype(F32), gq_ref[...])
    q2 = _split_heads(qn)
    sc = _qk(q2, kn_ref[...])
    row = qi * tq + lax.broadcasted_iota(I32, (tq, seq), 0)
    col = lax.broadcasted_iota(I32, (tq, seq), 1)
    bias = slope_ref[...] * jnp.abs(row - col).astype(F32)
    v = v_ref[...]

    def component(s):
        s = s - bias
        m = jnp.max(s, axis=-1, keepdims=True)
        p = jnp.exp(s - m)
        l = jnp.sum(p, axis=-1, keepdims=True)
        return jnp.dot(p.astype(BF16), v, preferred_element_type=F32) / l

    o = component(sc[:tq]) - lam * component(sc[tq:])
    ms = jnp.mean(o * o, axis=-1, keepdims=True)
    o_ref[...] = (o * lax.rsqrt(ms + RMS_EPS) * sub_ref[...] * (1.0 - lam_init)).astype(o_ref.dtype)


def _attn_a(qkv, gq, gk, a_lambda, a_subln, lam_init):
    nb, seq, _ = qkv.shape
    tq = 256
    per = SEG_W // LANES
    slopes = jnp.asarray(_alibi_slopes(A_HEADS)).reshape(A_HEADS, 1, 1)
    kern = functools.partial(_attn_a_kernel, lam_init=lam_init, tq=tq, seq=seq)
    return pl.pallas_call(
        kern,
        grid=(nb, A_HEADS, seq // tq),
        in_specs=[pl.BlockSpec((None, tq, LANES), lambda b, h, i: (b, i, SEG_AQ * per + h)),
                  pl.BlockSpec((None, seq, LANES), lambda b, h, i: (b, 0, SEG_AK * per + h)),
                  pl.BlockSpec((None, seq, LANES), lambda b, h, i: (b, 0, SEG_AV * per + h)),
                  pl.BlockSpec((1, LANES), lambda b, h, i: (0, 0)),
                  pl.BlockSpec((1, LANES), lambda b, h, i: (0, 0)),
                  pl.BlockSpec((4, HEAD_DIM), lambda b, h, i: (0, 0)),
                  pl.BlockSpec((1, LANES), lambda b, h, i: (0, 0)),
                  pl.BlockSpec((None, 1, 1), lambda b, h, i: (h, 0, 0))],
        out_specs=pl.BlockSpec((None, tq, LANES), lambda b, h, i: (b, i, h)),
        out_shape=jax.ShapeDtypeStruct((nb, seq, BRANCH_W), BF16),
        scratch_shapes=[pltpu.VMEM((seq, LANES), BF16)],
        compiler_params=_params("parallel", "arbitrary", "arbitrary"),
        name="attn_a",
    )(qkv, qkv, qkv, gq, gk, a_lambda, a_subln.reshape(1, LANES), slopes)


def _attn_b_kernel(q_ref, k_ref, v_ref, gq_ref, gk_ref, slope_ref, sink_ref, o_ref, k2_ref, v2_ref,
                   *, tq, nk, seq):
    j = pl.program_id(1)
    qi = pl.program_id(2)

    @pl.when(qi == 0)
    def _():
        half = (lax.broadcasted_iota(I32, (seq, LANES), 1) >= HEAD_DIM).astype(I32)
        own = half == j
        kn = _head_norm(k_ref[...].astype(F32), gk_ref[...])
        k2_ref[...] = jnp.where(own, kn, pltpu.roll(kn, HEAD_DIM, axis=1)).astype(BF16)
        v = v_ref[...].astype(F32)
        v2_ref[...] = jnp.where(own, v, pltpu.roll(v, HEAD_DIM, axis=1)).astype(BF16)

    ws = pl.multiple_of(jnp.clip(qi * tq - B_WINDOW, 0, seq - nk), LANES)
    kw = k2_ref[pl.ds(ws, nk), :]
    vw = v2_ref[pl.ds(ws, nk), :]
    q = q_ref[...].astype(F32)
    g = gq_ref[...]
    qa = _head_norm(q[:, :LANES], g)
    qb = _head_norm(q[:, LANES:], g)
    q4 = jnp.concatenate([_split_heads(qa), _split_heads(qb)], axis=0)
    sc = _qk(q4, kw)
    row = qi * tq + lax.broadcasted_iota(I32, (tq, nk), 0)
    col = ws + lax.broadcasted_iota(I32, (tq, nk), 1)
    rel = jnp.abs(row - col)
    valid = rel <= B_WINDOW
    relf = rel.astype(F32)
    outs = []
    for gi in range(B_HEADS // B_KV_HEADS):
        s = jnp.where(valid, sc[gi * tq:(gi + 1) * tq] - slope_ref[gi] * relf, NEG_INF)
        m = jnp.max(s, axis=-1, keepdims=True)
        p = jnp.exp(s - m)
        l = jnp.sum(p, axis=-1, keepdims=True) + jnp.exp(sink_ref[gi] - m)
        outs.append(jnp.dot(p.astype(BF16), vw, preferred_element_type=F32) / l)
    lo = _lo_lanes((tq, LANES))
    o = jnp.concatenate([jnp.where(lo, outs[0], outs[1]), jnp.where(lo, outs[2], outs[3])], axis=1)
    o_ref[...] = o.astype(o_ref.dtype)


def _attn_b(qkv, kvb, gq, gk, b_sink):
    nb, seq, _ = qkv.shape
    tq = 128
    nk = tq + 2 * B_WINDOW
    grp = B_HEADS // B_KV_HEADS
    qw = grp * HEAD_DIM
    slopes = jnp.asarray(_alibi_slopes(B_HEADS)).reshape(B_HEADS, 1, 1)
    kern = functools.partial(_attn_b_kernel, tq=tq, nk=nk, seq=seq)
    return pl.pallas_call(
        kern,
        grid=(nb, B_KV_HEADS, seq // tq),
        in_specs=[pl.BlockSpec((None, tq, qw), lambda b, j, i: (b, i, SEG_BQ * (SEG_W // qw) + j)),
                  pl.BlockSpec((None, seq, LANES), lambda b, j, i: (b, 0, 0)),
                  pl.BlockSpec((None, seq, LANES), lambda b, j, i: (b, 0, 1)),
                  pl.BlockSpec((1, LANES), lambda b, j, i: (0, 0)),
                  pl.BlockSpec((1, LANES), lambda b, j, i: (0, 0)),
                  pl.BlockSpec((grp, 1, 1), lambda b, j, i: (j, 0, 0)),
                  pl.BlockSpec((grp, 1, 1), lambda b, j, i: (j, 0, 0))],
        out_specs=pl.BlockSpec((None, tq, qw), lambda b, j, i: (b, i, j)),
        out_shape=jax.ShapeDtypeStruct((nb, seq, BRANCH_W), BF16),
        scratch_shapes=[pltpu.VMEM((seq, LANES), BF16), pltpu.VMEM((seq, LANES), BF16)],
        compiler_params=_params("parallel", "arbitrary", "arbitrary"),
        name="attn_b",
    )(qkv, kvb, kvb, gq, gk, slopes, b_sink.reshape(B_HEADS, 1, 1))


def _attn_c_kernel(q_ref, k_ref, v_ref, gq_ref, gk_ref, slope_ref, o_ref, lse_ref, kn_ref,
                   *, tq, nk, sub, side, dil):
    qi = pl.program_id(2)
    n_pairs = SEG_W // LANES

    @pl.when(qi == 0)
    def _():
        for hp in range(n_pairs):
            cs = slice(hp * LANES, (hp + 1) * LANES)
            kn_ref[:, cs] = _head_norm(k_ref[:, cs].astype(F32), gk_ref[...]).astype(BF16)

    ws = pl.multiple_of(jnp.clip(qi * tq - side, 0, sub - nk), HEAD_DIM)
    row = qi * tq + lax.broadcasted_iota(I32, (tq, nk), 0)
    col = ws + lax.broadcasted_iota(I32, (tq, nk), 1)
    rel = jnp.abs(row - col)
    valid = rel <= side
    relf = (dil * rel).astype(F32)
    lo = _lo_lanes((tq, LANES))
    for hp in range(n_pairs):
        cs = slice(hp * LANES, (hp + 1) * LANES)
        qn = _head_norm(q_ref[:, cs].astype(F32), gq_ref[...])
        sc = _qk(_split_heads(qn), kn_ref[pl.ds(ws, nk), cs])
        vw = v_ref[pl.ds(ws, nk), cs]
        o2, l2 = [], []
        for t in range(2):
            s = jnp.where(valid, sc[t * tq:(t + 1) * tq] - slope_ref[2 * hp + t] * relf, NEG_INF)
            m = jnp.max(s, axis=-1, keepdims=True)
            p = jnp.exp(s - m)
            l = jnp.sum(p, axis=-1, keepdims=True)
            o2.append(jnp.dot(p.astype(BF16), vw, preferred_element_type=F32) / l)
            l2.append(m + jnp.log(l))
        o_ref[:, cs] = jnp.where(lo, o2[0], o2[1]).astype(o_ref.dtype)
        lse_ref[:, cs] = jnp.where(lo, l2[0], l2[1])


def _attn_c_group(src, seg, gi, gq, gk):
    nb, dil, sub, _ = src.shape
    win, gdil = C_DILATIONS[gi]
    assert gdil == dil
    side = win // (2 * dil)
    tq = min(256, sub)
    nk = min(tq + 2 * side, sub)
    slopes = jnp.asarray(_alibi_slopes(N_DIL * C_HEADS).reshape(N_DIL, C_HEADS)[gi]).reshape(C_HEADS, 1, 1)
    kern = functools.partial(_attn_c_kernel, tq=tq, nk=nk, sub=sub, side=side, dil=dil)
    return pl.pallas_call(
        kern,
        grid=(nb, dil, sub // tq),
        in_specs=[pl.BlockSpec((None, None, tq, SEG_W), lambda b, r, i: (b, r, i, seg)),
                  pl.BlockSpec((None, None, sub, SEG_W), lambda b, r, i: (b, r, 0, seg + 1)),
                  pl.BlockSpec((None, None, sub, SEG_W), lambda b, r, i: (b, r, 0, seg + 2)),
                  pl.BlockSpec((1, LANES), lambda b, r, i: (0, 0)),
                  pl.BlockSpec((1, LANES), lambda b, r, i: (0, 0)),
                  pl.BlockSpec((C_HEADS, 1, 1), lambda b, r, i: (0, 0, 0))],
        out_specs=[pl.BlockSpec((None, None, tq, SEG_W), lambda b, r, i: (b, r, i, 0)),
                   pl.BlockSpec((None, None, tq, SEG_W), lambda b, r, i: (b, r, i, 0))],
        out_shape=[jax.ShapeDtypeStruct((nb, dil, sub, SEG_W), BF16),
                   jax.ShapeDtypeStruct((nb, dil, sub, SEG_W), F32)],
        scratch_shapes=[pltpu.VMEM((sub, SEG_W), BF16)],
        compiler_params=_params("parallel", "arbitrary", "arbitrary"),
        name=f"attn_c{gi}",
    )(src, src, src, gq, gk, slopes)


def _attn_d_kernel(q_ref, k_ref, v_ref, gq_ref, gk_ref, bias_ref, o_ref, kn_ref, *, rows, nk):
    r = pl.program_id(1)
    n_pairs = SEG_W // LANES

    @pl.when(r == 0)
    def _():
        for hp in range(n_pairs):
            cs = slice(hp * LANES, (hp + 1) * LANES)
            kn_ref[:, cs] = _head_norm(k_ref[:, cs].astype(F32), gk_ref[...]).astype(BF16)

    kr = nk // GRID_W
    ws = pl.multiple_of(jnp.clip(r - kr // 2, 0, rows - kr) * GRID_W, GRID_W)
    lo = _lo_lanes((GRID_W, LANES))
    for hp in range(n_pairs):
        cs = slice(hp * LANES, (hp + 1) * LANES)
        qn = _head_norm(q_ref[:, cs].astype(F32), gq_ref[...])
        sc = _qk(_split_heads(qn), kn_ref[pl.ds(ws, nk), cs])
        vw = v_ref[pl.ds(ws, nk), cs]
        o2 = []
        for t in range(2):
            s = sc[t * GRID_W:(t + 1) * GRID_W] + bias_ref[2 * hp + t]
            m = jnp.max(s, axis=-1, keepdims=True)
            p = jnp.exp(s - m)
            l = jnp.sum(p, axis=-1, keepdims=True)
            o2.append(jnp.dot(p.astype(BF16), vw, preferred_element_type=F32) / l)
        o_ref[:, cs] = jnp.where(lo, o2[0], o2[1]).astype(o_ref.dtype)


def _na_bias_table(rpb, rows):
    kr = min(NA_ROWS, rows)
    qc = np.arange(GRID_W)[:, None]
    kc = np.arange(GRID_W)[None, :]
    col_start = np.clip(qc - NA_COLS // 2, 0, GRID_W - NA_COLS)
    ok = (kc >= col_start) & (kc < col_start + NA_COLS)
    dc = np.clip(kc - qc + NA_COLS - 1, 0, 2 * NA_COLS - 2)
    drow = np.arange(kr)[:, None] + np.arange(kr)[None, :]
    t = rpb[:, drow][:, :, :, dc]
    t = jnp.where(ok[None, None, None], t.astype(F32), NEG_INF)
    t = t.transpose(1, 0, 3, 2, 4)
    return t.reshape(kr, D_HEADS, GRID_W, kr * GRID_W)


def _attn_d(qkv, gq, gk, d_rpb):
    nb, seq, _ = qkv.shape
    rows = seq // GRID_W
    kr = min(NA_ROWS, rows)
    nk = kr * GRID_W
    table = _na_bias_table(d_rpb, rows)

    def d0(r):
        return jnp.clip(r - kr // 2, 0, rows - kr) - r + NA_ROWS - 1

    kern = functools.partial(_attn_d_kernel, rows=rows, nk=nk)
    return pl.pallas_call(
        kern,
        grid=(nb, rows),
        in_specs=[pl.BlockSpec((None, GRID_W, SEG_W), lambda b, r: (b, r, SEG_DQ)),
                  pl.BlockSpec((None, seq, SEG_W), lambda b, r: (b, 0, SEG_DK)),
                  pl.BlockSpec((None, seq, SEG_W), lambda b, r: (b, 0, SEG_DV)),
                  pl.BlockSpec((1, LANES), lambda b, r: (0, 0)),
                  pl.BlockSpec((1, LANES), lambda b, r: (0, 0)),
                  pl.BlockSpec((None, D_HEADS, GRID_W, nk), lambda b, r: (d0(r), 0, 0, 0))],
        out_specs=pl.BlockSpec((None, GRID_W, SEG_W), lambda b, r: (b, r, 0)),
        out_shape=jax.ShapeDtypeStruct((nb, seq, BRANCH_W), BF16),
        scratch_shapes=[pltpu.VMEM((seq, SEG_W), BF16)],
        compiler_params=_params("parallel", "arbitrary"),
        name="attn_d",
    )(qkv, qkv, qkv, gq, gk, table)


def _to_token_order(blk_ref, scr_ref):
    dil, n, _ = blk_ref.shape
    for r in range(dil):
        for c in range(SEG_W // LANES):
            scr_ref[c, pl.ds(r, n, stride=dil), :] = blk_ref[r, :, c * LANES:(c + 1) * LANES].astype(F32)


def _merge_kernel(oa_ref, ob_ref, oc0_ref, oc1_ref, oc2_ref, l0_ref, l1_ref, l2_ref, od_ref,
                  gates_ref, wb_ref, o_ref, s_o1, s_l1, s_o2, s_l2):
    _to_token_order(oc1_ref, s_o1)
    _to_token_order(l1_ref, s_l1)
    _to_token_order(oc2_ref, s_o2)
    _to_token_order(l2_ref, s_l2)
    slabs = []
    for c in range(SEG_W // LANES):
        cs = slice(c * LANES, (c + 1) * LANES)
        l0, l1, l2 = l0_ref[:, cs], s_l1[c], s_l2[c]
        lm = jnp.maximum(jnp.maximum(l0, l1), l2)
        e0, e1, e2 = jnp.exp(l0 - lm), jnp.exp(l1 - lm), jnp.exp(l2 - lm)
        oc = (e0 * oc0_ref[:, cs].astype(F32) + e1 * s_o1[c] + e2 * s_o2[c]) / (e0 + e1 + e2)
        slabs.append(oc.astype(BF16))
    branches = (oa_ref[...], ob_ref[...], jnp.concatenate(slabs, axis=1), od_ref[...])
    acc = None
    for n, br in enumerate(branches):
        proj = jnp.dot(br, wb_ref[n], preferred_element_type=F32)
        term = gates_ref[:, n * D_MODEL:(n + 1) * D_MODEL].astype(F32) * proj
        acc = term if acc is None else acc + term
    o_ref[...] = acc.astype(o_ref.dtype)


def _merge(oa, ob, oc, lses, od, gates, w_branch):
    nb, seq, _ = oa.shape
    tm = 256
    per = seq // tm
    tok = pl.BlockSpec((None, tm, BRANCH_W), lambda b, i: (b, i, 0))

    def res(dil):
        return pl.BlockSpec((None, dil, tm // dil, BRANCH_W), lambda b, i: (b, 0, i, 0))

    dils = [d for _, d in C_DILATIONS]
    assert dils[0] == 1
    res0 = pl.BlockSpec((None, None, tm, BRANCH_W), lambda b, i: (b, 0, i, 0))
    slab = pltpu.VMEM((BRANCH_W // LANES, tm, LANES), F32)
    return pl.pallas_call(
        _merge_kernel,
        grid=(nb, per),
        in_specs=[tok, tok, res0, res(dils[1]), res(dils[2]), res0, res(dils[1]), res(dils[2]), tok,
                  pl.BlockSpec((tm, N_BRANCH * D_MODEL), lambda b, i: (b * per + i, 0)),
                  pl.BlockSpec((N_BRANCH, BRANCH_W, D_MODEL), lambda b, i: (0, 0, 0))],
        out_specs=pl.BlockSpec((tm, D_MODEL), lambda b, i: (b * per + i, 0)),
        out_shape=jax.ShapeDtypeStruct((nb * seq, D_MODEL), BF16),
        scratch_shapes=[slab, slab, slab, slab],
        compiler_params=_params("parallel", "arbitrary"),
        name="merge",
    )(oa, ob, oc[0], oc[1], oc[2], lses[0], lses[1], lses[2], od, gates, w_branch)


def _route_kernel(x_ref, g_ref, sc_ref, sh_ref, wr_ref, br_ref, hf_ref, idx_ref, p_ref):
    h = _modulated_norm(x_ref[...], g_ref, sc_ref, sh_ref)
    _store_row_tiles(hf_ref, h)
    logits = jnp.dot(h.astype(BF16), wr_ref[...], preferred_element_type=F32) + br_ref[...]
    lane = lax.broadcasted_iota(I32, logits.shape, 1)
    idx_t = jnp.zeros(logits.shape, I32)
    e_t = jnp.zeros(logits.shape, F32)
    den = None
    top = None
    for k in range(TOP_K):
        m = jnp.max(logits, axis=-1, keepdims=True)
        idx = jnp.min(jnp.where(logits == m, lane, LANES), axis=-1, keepdims=True)
        logits = jnp.where(lane == idx, -jnp.inf, logits)
        if k == 0:
            top = m
        e = jnp.exp(m - top)
        den = e if den is None else den + e
        idx_t = jnp.where(lane == k, idx, idx_t)
        e_t = jnp.where(lane == k, e, e_t)
    idx_ref[...] = idx_t
    p_ref[...] = e_t / den


def _route(x, g, sc, sh, w_router, b_router):
    nb, s, d = x.shape
    tm = 256
    wr = jnp.zeros((d, LANES), BF16).at[:, :N_EXPERTS].set(w_router.astype(BF16))
    br = jnp.full((1, LANES), NEG_INF, F32).at[0, :N_EXPERTS].set(b_router)
    per = s // tm
    return pl.pallas_call(
        _route_kernel,
        grid=(nb, per),
        in_specs=[pl.BlockSpec((None, tm, d), lambda b, i: (b, i, 0)),
                  pl.BlockSpec((1, d), lambda b, i: (0, 0)),
                  pl.BlockSpec((None, 1, d), lambda b, i: (b, 0, 0)),
                  pl.BlockSpec((None, 1, d), lambda b, i: (b, 0, 0)),
                  pl.BlockSpec((d, LANES), lambda b, i: (0, 0)),
                  pl.BlockSpec((1, LANES), lambda b, i: (0, 0))],
        out_specs=[pl.BlockSpec((tm * ROW_TILE, LANES), lambda b, i: (b * per + i, 0)),
                   pl.BlockSpec((tm, LANES), lambda b, i: (b * per + i, 0)),
                   pl.BlockSpec((tm, LANES), lambda b, i: (b * per + i, 0))],
        out_shape=[jax.ShapeDtypeStruct((nb * s * ROW_TILE, LANES), U32),
                   jax.ShapeDtypeStruct((nb * s, LANES), I32),
                   jax.ShapeDtypeStruct((nb * s, LANES), F32)],
        compiler_params=_params("parallel", "parallel"),
        name="route",
    )(x, g.reshape(1, d), sc, sh, wr, br)


def _tile_rows(t):
    return pl.ds(pl.multiple_of(t * ROW_TILE, ROW_TILE), ROW_TILE)


def _gather_kernel(nrows_ref, tok_ref, src_ref, o_ref, sem, *, rows):
    i = pl.program_id(0)

    def row_copy(k, t):
        return pltpu.make_async_copy(src_ref.at[_tile_rows(t), :], o_ref.at[_tile_rows(k), :], sem)

    @pl.when(i * rows < nrows_ref[0])
    def _():
        def issue(k, c):
            row_copy(k, tok_ref[0, 0, k]).start()
            return c
        lax.fori_loop(0, rows, issue, 0, unroll=DMA_UNROLL)

        def drain(k, c):
            row_copy(k, 0).wait()
            return c
        lax.fori_loop(0, rows, drain, 0, unroll=DMA_UNROLL)

    @pl.when(i * rows >= nrows_ref[0])
    def _():
        o_ref[...] = jnp.zeros(o_ref.shape, o_ref.dtype)


def _gather_rows(src, slot_tok, n_rows_used):
    n_slots = slot_tok.shape[0]
    rows = GATHER_ROWS
    nblk = n_slots // rows
    kern = functools.partial(_gather_kernel, rows=rows)
    return pl.pallas_call(
        kern,
        grid_spec=pltpu.PrefetchScalarGridSpec(
            num_scalar_prefetch=1,
            grid=(nblk,),
            in_specs=[pl.BlockSpec((1, 1, rows), lambda i, n: (i, 0, 0), memory_space=pltpu.SMEM),
                      pl.BlockSpec(memory_space=pl.ANY)],
            out_specs=pl.BlockSpec((rows * ROW_TILE, LANES), lambda i, n: (i, 0)),
            scratch_shapes=[pltpu.SemaphoreType.DMA(())],
        ),
        out_shape=jax.ShapeDtypeStruct((n_slots * ROW_TILE, LANES), U32),
        compiler_params=_params("arbitrary"),
        name="moe_gather",
    )(n_rows_used, slot_tok.reshape(nblk, 1, rows), src)


def _expert_kernel(be_ref, nu_ref, xs_ref, wg_ref, wu_ref, wdl_ref, wdh_ref, bg_ref, bu_ref, bdl_ref, bdh_ref,
                   o_ref, xb_ref, a_ref, *, n_up, n_down):
    i = pl.program_id(0)
    f = pl.program_id(1)
    used = i < nu_ref[0]
    rows = xb_ref.shape[0]
    tf = wg_ref.shape[1]

    @pl.when(used & (f == 0))
    def _():
        for c in range(ROW_TILE):
            lo, hi = _load_row_tile_chunk(xs_ref, c)
            xb_ref[:, c * LANES:(c + 1) * LANES] = lo.astype(BF16)
            xb_ref[:, HALF_D + c * LANES:HALF_D + (c + 1) * LANES] = hi.astype(BF16)

    @pl.when(used & (f < n_up))
    def _():
        x = xb_ref[...]
        g = jnp.dot(x, wg_ref[...].astype(BF16), preferred_element_type=F32) + bg_ref[...]
        u = jnp.dot(x, wu_ref[...].astype(BF16), preferred_element_type=F32) + bu_ref[...]
        g = jnp.minimum(g, SWIGLU_LIMIT)
        u = jnp.clip(u, -SWIGLU_LIMIT, SWIGLU_LIMIT)
        a = ((u + 1.0) * g * jax.nn.sigmoid(SWIGLU_ALPHA * g)).astype(BF16)
        for ff in range(n_up):
            @pl.when(f == ff)
            def _(ff=ff):
                a_ref[:, ff * tf:(ff + 1) * tf] = a

    @pl.when(used & (f >= n_up))
    def _():
        m = f - n_up
        a = a_ref[...]
        lo = jnp.dot(a, wdl_ref[...].astype(BF16), preferred_element_type=F32) + bdl_ref[...]
        hi = jnp.dot(a, wdh_ref[...].astype(BF16), preferred_element_type=F32) + bdh_ref[...]
        packed = _pack_pair(lo, hi)
        for j in range(packed.shape[1] // LANES):
            c = m * (packed.shape[1] // LANES) + j
            o_ref[pl.ds(c, rows, stride=ROW_TILE), :] = packed[:, j * LANES:(j + 1) * LANES]

    @pl.when(jnp.logical_not(used) & (f == n_up + n_down - 1))
    def _():
        o_ref[...] = jnp.zeros(o_ref.shape, o_ref.dtype)


def _experts(xs, block_expert, n_used, layer, w_gate_up, b_gate_up, w_down, b_down):
    rows, tf, tn = MOE_ROWS, MOE_TF, MOE_TN
    n_slots = xs.shape[0] // ROW_TILE
    nblk = n_slots // rows
    n_up = D_FF // tf
    n_down = HALF_D // tn
    depth, e = w_gate_up.shape[:2]
    kern = functools.partial(_expert_kernel, n_up=n_up, n_down=n_down)

    def up(i, f, nu):
        return jnp.where(i < nu[0], jnp.minimum(f, n_up - 1), n_up - 1)

    def down(i, f, nu):
        return jnp.where(i < nu[0], jnp.maximum(f - n_up, 0), n_down - 1)

    return pl.pallas_call(
        kern,
        grid_spec=pltpu.PrefetchScalarGridSpec(
            num_scalar_prefetch=2,
            grid=(nblk, n_up + n_down),
            in_specs=[pl.BlockSpec((rows * ROW_TILE, LANES), lambda i, f, be, nu: (i, 0)),
                      pl.BlockSpec((None, None, D_MODEL, tf), lambda i, f, be, nu: (layer, be[i], 0, up(i, f, nu))),
                      pl.BlockSpec((None, None, D_MODEL, tf),
                                   lambda i, f, be, nu: (layer, be[i], 0, n_up + up(i, f, nu))),
                      pl.BlockSpec((None, None, D_FF, tn), lambda i, f, be, nu: (layer, be[i], 0, down(i, f, nu))),
                      pl.BlockSpec((None, None, D_FF, tn),
                                   lambda i, f, be, nu: (layer, be[i], 0, n_down + down(i, f, nu))),
                      pl.BlockSpec((None, None, 1, tf), lambda i, f, be, nu: (layer, be[i], 0, up(i, f, nu))),
                      pl.BlockSpec((None, None, 1, tf),
                                   lambda i, f, be, nu: (layer, be[i], 0, n_up + up(i, f, nu))),
                      pl.BlockSpec((None, None, 1, tn), lambda i, f, be, nu: (layer, be[i], 0, down(i, f, nu))),
                      pl.BlockSpec((None, None, 1, tn),
                                   lambda i, f, be, nu: (layer, be[i], 0, n_down + down(i, f, nu)))],
            out_specs=pl.BlockSpec((rows * ROW_TILE, LANES), lambda i, f, be, nu: (i, 0)),
            scratch_shapes=[pltpu.VMEM((rows, D_MODEL), BF16), pltpu.VMEM((rows, D_FF), BF16)],
        ),
        out_shape=jax.ShapeDtypeStruct((n_slots * ROW_TILE, LANES), U32),
        compiler_params=_params("arbitrary", "arbitrary"),
        name="moe_experts",
    )(block_expert, n_used, xs, w_gate_up, w_gate_up, w_down, w_down,
      b_gate_up.reshape(depth, e, 1, 2 * D_FF), b_gate_up.reshape(depth, e, 1, 2 * D_FF),
      b_down.reshape(depth, e, 1, D_MODEL), b_down.reshape(depth, e, 1, D_MODEL))


def _combine_kernel(dest_ref, src_ref, p_ref, x_ref, g_ref, o_ref, buf_ref, sem, *, rows):
    def row_copy(k, t, d):
        return pltpu.make_async_copy(src_ref.at[_tile_rows(d), :], buf_ref.at[k, _tile_rows(t), :], sem)

    for k in range(TOP_K):
        def issue(t, c, k=k):
            row_copy(k, t, dest_ref[0, 0, t * TOP_K + k]).start()
            return c
        lax.fori_loop(0, rows, issue, 0, unroll=DMA_UNROLL)
    for k in range(TOP_K):
        def drain(t, c, k=k):
            row_copy(k, t, 0).wait()
            return c
        lax.fori_loop(0, rows, drain, 0, unroll=DMA_UNROLL)

    p = p_ref[...]
    pk = [p[:, k:k + 1] for k in range(TOP_K)]
    for c in range(ROW_TILE):
        y_lo = None
        y_hi = None
        for k in range(TOP_K):
            lo, hi = _load_row_tile_chunk(buf_ref.at[k], c)
            y_lo = pk[k] * lo if y_lo is None else y_lo + pk[k] * lo
            y_hi = pk[k] * hi if y_hi is None else y_hi + pk[k] * hi
        cl = slice(c * LANES, (c + 1) * LANES)
        ch = slice(HALF_D + c * LANES, HALF_D + (c + 1) * LANES)
        o_ref[:, cl] = x_ref[:, cl] + g_ref[:, cl] * y_lo
        o_ref[:, ch] = x_ref[:, ch] + g_ref[:, ch] * y_hi


def _combine(outs, dest, probs, x, gate, seq):
    t, d = x.shape
    rows = COMBINE_ROWS
    nblk = t // rows
    per = seq // rows
    kern = functools.partial(_combine_kernel, rows=rows)
    return pl.pallas_call(
        kern,
        grid=(nblk,),
        in_specs=[pl.BlockSpec((1, 1, rows * TOP_K), lambda i: (i, 0, 0), memory_space=pltpu.SMEM),
                  pl.BlockSpec(memory_space=pl.ANY),
                  pl.BlockSpec((rows, LANES), lambda i: (i, 0)),
                  pl.BlockSpec((rows, d), lambda i: (i, 0)),
                  pl.BlockSpec((None, 1, d), lambda i: (i // per, 0, 0))],
        out_specs=pl.BlockSpec((rows, d), lambda i: (i, 0)),
        out_shape=jax.ShapeDtypeStruct((t, d), F32),
        scratch_shapes=[pltpu.VMEM((TOP_K, rows * ROW_TILE, LANES), U32), pltpu.SemaphoreType.DMA(())],
        compiler_params=_params("arbitrary"),
        name="moe_combine",
    )(dest.reshape(nblk, 1, rows * TOP_K), outs, probs, x, gate)


def _routing_plan(top_idx, n_tok):
    expert = top_idx.reshape(-1)
    token = jnp.arange(n_tok * TOP_K, dtype=I32) // TOP_K
    onehot = jax.nn.one_hot(expert, N_EXPERTS, dtype=I32)
    csum = jnp.cumsum(onehot, axis=0)
    rank = jnp.take_along_axis(csum, expert[:, None], axis=1)[:, 0] - 1
    padded = (csum[-1] + MOE_ROWS - 1) // MOE_ROWS * MOE_ROWS
    pad_end = jnp.cumsum(padded)
    dest = (pad_end[expert] - padded[expert] + rank).astype(I32)
    n_blocks = -(-(n_tok * TOP_K) // MOE_ROWS) + N_EXPERTS
    n_slots = n_blocks * MOE_ROWS
    slot_tok = jnp.zeros((n_slots,), I32).at[dest].set(token)
    block_expert = jnp.minimum(
        jnp.searchsorted(pad_end, jnp.arange(n_blocks, dtype=I32) * MOE_ROWS, side='right'),
        N_EXPERTS - 1).astype(I32)
    n_rows_used = pad_end[-1:].astype(I32)
    return dest, slot_tok, block_expert, n_rows_used


def _moe(x, g, sc, sh, gate, layer, w_router, b_router, w_gate_up, b_gate_up, w_down, b_down):
    nb, seq, d = x.shape
    n_tok = nb * seq
    hf, idx_t, p_t = _route(x, g, sc, sh, w_router, b_router)
    dest, slot_tok, block_expert, n_rows_used = _routing_plan(idx_t[:, :TOP_K], n_tok)
    xs = _gather_rows(hf, slot_tok, n_rows_used)
    outs = _experts(xs, block_expert, n_rows_used // MOE_ROWS, layer, w_gate_up, b_gate_up, w_down, b_down)
    y = _combine(outs, dest, p_t, x.reshape(n_tok, d), gate, seq)
    return y.reshape(nb, seq, d)


def _pair_gain(g, scale=1.0):
    return (jnp.tile(g.astype(F32), LANES // HEAD_DIM) * scale).reshape(1, LANES)


def _token_mixer(x, layer, lam_init, g, sc, sh, gate, p):
    nb, seq, d = x.shape
    n_tok = nb * seq
    hm = _normmod(x, g, sc, sh).reshape(n_tok, d)
    w_in = p['w_in'][layer]
    w_main = jnp.concatenate([w_in[:, :BKV_LO], w_in[:, C_LO:C_LO + C_GROUP_W], w_in[:, D_LO:]],
                             axis=1).astype(BF16)
    qkv = _matmul(hm, w_main, "in_proj").reshape(nb, seq, QKV_W)
    kvb = _matmul(hm, w_in[:, BKV_LO:BKV_HI].astype(BF16), "in_proj_bkv").reshape(nb, seq, BKV_HI - BKV_LO)
    c_src = [(qkv.reshape(nb, 1, seq, QKV_W), SEG_C0)]
    for gi in range(1, N_DIL):
        lo = C_LO + gi * C_GROUP_W
        c_src.append((_matmul_by_residue(hm, w_in[:, lo:lo + C_GROUP_W].astype(BF16), nb, seq,
                                         C_DILATIONS[gi][1], f"in_proj_c{gi}"), 0))
    qs = HEAD_DIM ** -0.5
    oa = _attn_a(qkv, _pair_gain(p['a_qk_norm'][layer, 0], qs), _pair_gain(p['a_qk_norm'][layer, 1]),
                 p['a_lambda'][layer], p['a_subln'][layer], lam_init)
    ob = _attn_b(qkv, kvb, _pair_gain(p['b_qk_norm'][layer, 0], qs), _pair_gain(p['b_qk_norm'][layer, 1]),
                 p['b_sink'][layer])
    ocs, lses = [], []
    for gi in range(N_DIL):
        o, lse = _attn_c_group(c_src[gi][0], c_src[gi][1], gi, _pair_gain(p['c_qk_norm'][layer, gi, 0], qs),
                               _pair_gain(p['c_qk_norm'][layer, gi, 1]))
        ocs.append(o)
        lses.append(lse)
    od = _attn_d(qkv, _pair_gain(p['d_qk_norm'][layer, 0], qs), _pair_gain(p['d_qk_norm'][layer, 1]),
                 p['d_rpb'][layer])
    gates = _matmul_sigmoid(hm, p['w_gate'][layer].astype(BF16), p['b_gate'][layer], "gate_proj")
    mixed = _merge(oa, ob, ocs, lses, od, gates, p['w_branch'][layer].astype(BF16))
    y = _matmul_residual(mixed, p['w_out'][layer].astype(BF16), x.reshape(n_tok, d), gate, seq, "out_proj")
    return y.reshape(nb, seq, d)


def kernel(x_prompt, x_sample, c_prompt, c_sample, w_ada, b_ada, norm_g, w_in, a_qk_norm, a_lambda, a_subln, b_qk_norm, b_sink, c_qk_norm, d_qk_norm, d_rpb, w_gate, b_gate, w_branch, w_out, w_router, b_router, w_gate_up, b_gate_up, w_down, b_down):
    p = dict(w_in=w_in, a_qk_norm=a_qk_norm, a_lambda=a_lambda, a_subln=a_subln, b_qk_norm=b_qk_norm,
             b_sink=b_sink, c_qk_norm=c_qk_norm, d_qk_norm=d_qk_norm, d_rpb=d_rpb, w_gate=w_gate,
             b_gate=b_gate, w_branch=w_branch, w_out=w_out)
    assert x_prompt.shape[1:] == x_sample.shape[1:]
    n_prompt = x_prompt.shape[0]
    x = jnp.concatenate([x_prompt, x_sample], axis=0)
    c = jnp.concatenate([c_prompt, c_sample], axis=0)
    nb, seq, d = x.shape
    depth = w_ada.shape[0]
    mod = _ada_mod(c, w_ada, b_ada).reshape(depth, nb, 6, 1, d)
    for layer in range(depth):
        sh1, sc1, g1, sh2, sc2, g2 = (mod[layer, :, i] for i in range(6))
        lam_init = 0.8 - 0.6 * math.exp(-0.3 * layer)
        x = _token_mixer(x, layer, lam_init, norm_g[layer, 0], sc1, sh1, g1, p)
        x = _moe(x, norm_g[layer, 1], sc2, sh2, g2, layer, w_router[layer], b_router[layer],
                 w_gate_up, b_gate_up, w_down, b_down)
    return x[:n_prompt], x[n_prompt:]
```

```python
import functools
import math

import numpy as np
import jax
import jax.numpy as jnp
from jax import lax
from jax.experimental import pallas as pl
from jax.experimental.pallas import tpu as pltpu

F32 = jnp.float32
BF16 = jnp.bfloat16
U32 = jnp.uint32
I32 = jnp.int32

D_MODEL = 2048
DEPTH = 4
HEAD_DIM = 64
N_BRANCH = 4
BRANCH_W = D_MODEL // N_BRANCH
A_HEADS = 4
B_HEADS = 8
B_KV_HEADS = 2
B_WINDOW = 128
C_HEADS = 8
C_DILATIONS = ((128, 1), (512, 4), (2048, 16))
N_DIL = 3
GRID_W = 64
D_HEADS = 8
NA_ROWS = 8
NA_COLS = 16
N_EXPERTS = 32
TOP_K = 4
D_FF = D_MODEL
SWIGLU_ALPHA = 1.702
SWIGLU_LIMIT = 7.0
RMS_EPS = 1e-6
NEG_INF = -1e30

LANES = 128
HALF_D = D_MODEL // 2
SEG_W = 512
SEG_AQ, SEG_AK, SEG_AV, SEG_BQ = 0, 1, 2, 3
SEG_C0 = 4
SEG_DQ, SEG_DK, SEG_DV = 7, 8, 9
N_SEG = 10
QKV_W = N_SEG * SEG_W
BKV_LO, BKV_HI = 2048, 2304
C_LO = 2304
C_GROUP_W = 3 * SEG_W
D_LO = C_LO + N_DIL * C_GROUP_W
ROW_TILE = 8

VMEM_LIMIT = 56 * 1024 * 1024

MOE_ROWS = 1024
MOE_TF = 256
MOE_TN = 256
DISPATCH_ROWS = 256
COMBINE_ROWS = 256
DMA_UNROLL = 8
DMA_QUEUES = 2


def _params(*sem):
    return pltpu.CompilerParams(dimension_semantics=sem, vmem_limit_bytes=VMEM_LIMIT)


def _alibi_slopes(n):
    return (2.0 ** (-8.0 * np.arange(1, n + 1, dtype=np.float64) / n)).astype(np.float32)


def _group_mean_sq(x):
    r = lax.broadcasted_iota(I32, (LANES, LANES), 0) // HEAD_DIM
    c = lax.broadcasted_iota(I32, (LANES, LANES), 1) // HEAD_DIM
    ones = jnp.where(r == c, 1.0, 0.0).astype(BF16)
    return jnp.dot((x * x).astype(BF16), ones, preferred_element_type=F32) * (1.0 / HEAD_DIM)


def _head_norm(x, gain):
    return x * lax.rsqrt(_group_mean_sq(x) + RMS_EPS) * gain


def _lo_lanes(shape):
    return lax.broadcasted_iota(I32, shape, 1) < HEAD_DIM


def _split_heads(x):
    lo = _lo_lanes(x.shape)
    return jnp.concatenate([jnp.where(lo, x, 0.0), jnp.where(lo, 0.0, x)], axis=0).astype(BF16)


def _qk(q, k):
    return lax.dot_general(q, k, (((1,), (1,)), ((), ())), preferred_element_type=F32)


def _pack_pair(a, b):
    ua = lax.bitcast_convert_type(a.astype(BF16).astype(F32), U32)
    ub = lax.bitcast_convert_type(b.astype(BF16).astype(F32), U32)
    return (ua >> 16) | (ub & jnp.uint32(0xFFFF0000))


def _unpack_pair(w):
    a = lax.bitcast_convert_type(w << 16, F32)
    b = lax.bitcast_convert_type(w & jnp.uint32(0xFFFF0000), F32)
    return a, b


def _store_row_tiles(ref, val):
    n = val.shape[0]
    for c in range(ROW_TILE):
        lo = val[:, c * LANES:(c + 1) * LANES]
        hi = val[:, HALF_D + c * LANES:HALF_D + (c + 1) * LANES]
        ref[pl.ds(c, n, stride=ROW_TILE), :] = _pack_pair(lo, hi)


def _load_row_tile_chunk(ref, c):
    n = ref.shape[0] // ROW_TILE
    return _unpack_pair(ref[pl.ds(c, n, stride=ROW_TILE), :])


def _ada_kernel(c_ref, w_ref, b_ref, o_ref):
    c = c_ref[...]
    h = (c * jax.nn.sigmoid(c)).astype(BF16)
    o_ref[...] = jnp.dot(h, w_ref[...].astype(BF16), preferred_element_type=F32) + b_ref[...]


def _ada_mod(c, w_ada, b_ada):
    nb, d = c.shape
    depth, _, n = w_ada.shape
    tn = 1024
    return pl.pallas_call(
        _ada_kernel,
        grid=(depth, n // tn),
        in_specs=[pl.BlockSpec((nb, d), lambda l, j: (0, 0)),
                  pl.BlockSpec((None, d, tn), lambda l, j: (l, 0, j)),
                  pl.BlockSpec((None, 1, tn), lambda l, j: (l, 0, j))],
        out_specs=pl.BlockSpec((None, nb, tn), lambda l, j: (l, 0, j)),
        out_shape=jax.ShapeDtypeStruct((depth, nb, n), F32),
        compiler_params=_params("parallel", "parallel"),
        name="ada_mod",
    )(c, w_ada, b_ada.reshape(depth, 1, n))


def _modulated_norm(x, g_ref, sc_ref, sh_ref):
    ms = jnp.mean(x * x, axis=-1, keepdims=True)
    y = x * lax.rsqrt(ms + RMS_EPS) * g_ref[...]
    return y * (1.0 + sc_ref[...]) + sh_ref[...]


def _normmod_kernel(x_ref, g_ref, sc_ref, sh_ref, o_ref):
    o_ref[...] = _modulated_norm(x_ref[...], g_ref, sc_ref, sh_ref).astype(o_ref.dtype)


def _normmod(x, g, sc, sh):
    nb, s, d = x.shape
    tm = 512
    return pl.pallas_call(
        _normmod_kernel,
        grid=(nb, s // tm),
        in_specs=[pl.BlockSpec((None, tm, d), lambda b, i: (b, i, 0)),
                  pl.BlockSpec((1, d), lambda b, i: (0, 0)),
                  pl.BlockSpec((None, 1, d), lambda b, i: (b, 0, 0)),
                  pl.BlockSpec((None, 1, d), lambda b, i: (b, 0, 0))],
        out_specs=pl.BlockSpec((None, tm, d), lambda b, i: (b, i, 0)),
        out_shape=jax.ShapeDtypeStruct((nb, s, d), BF16),
        compiler_params=_params("parallel", "parallel"),
        name="normmod",
    )(x, g.reshape(1, d), sc, sh)


def _mm_kernel(x_ref, w_ref, o_ref):
    o_ref[...] = jnp.dot(x_ref[...], w_ref[...], preferred_element_type=F32).astype(o_ref.dtype)


def _mm_sigmoid_kernel(x_ref, w_ref, b_ref, o_ref):
    acc = jnp.dot(x_ref[...], w_ref[...], preferred_element_type=F32) + b_ref[...]
    o_ref[...] = jax.nn.sigmoid(acc).astype(o_ref.dtype)


def _mm_resid_kernel(x_ref, w_ref, r_ref, g_ref, o_ref):
    acc = jnp.dot(x_ref[...], w_ref[...], preferred_element_type=F32)
    o_ref[...] = r_ref[...] + g_ref[...] * acc


def _mm_tiles(m, n):
    tm = min(1024, m)
    tn = 1024 if n % 1024 == 0 else min(512, n)
    assert m % tm == 0 and n % tn == 0
    return tm, tn


def _matmul(x, w, name):
    m, k = x.shape
    n = w.shape[1]
    tm, tn = _mm_tiles(m, n)
    return pl.pallas_call(
        _mm_kernel,
        grid=(m // tm, n // tn),
        in_specs=[pl.BlockSpec((tm, k), lambda i, j: (i, 0)),
                  pl.BlockSpec((k, tn), lambda i, j: (0, j))],
        out_specs=pl.BlockSpec((tm, tn), lambda i, j: (i, j)),
        out_shape=jax.ShapeDtypeStruct((m, n), BF16),
        compiler_params=_params("parallel", "arbitrary"),
        name=name,
    )(x, w)


def _mm_residue_kernel(x_ref, w_ref, o_ref, acc_ref, *, dil):
    tm, tn = acc_ref.shape[1], acc_ref.shape[0] * LANES
    acc = jnp.dot(x_ref[...], w_ref[...], preferred_element_type=F32)
    for c in range(tn // LANES):
        acc_ref[c] = acc[:, c * LANES:(c + 1) * LANES]
    for r in range(dil):
        for c in range(tn // LANES):
            o_ref[r, :, c * LANES:(c + 1) * LANES] = acc_ref[c, pl.ds(r, tm // dil, stride=dil), :].astype(o_ref.dtype)


def _matmul_by_residue(x, w, nb, seq, dil, name):
    m, k = x.shape
    n = w.shape[1]
    tm, tn = _mm_tiles(m, n)
    assert seq % tm == 0 and tm % (dil * 16) == 0
    per = seq // tm
    kern = functools.partial(_mm_residue_kernel, dil=dil)
    return pl.pallas_call(
        kern,
        grid=(m // tm, n // tn),
        in_specs=[pl.BlockSpec((tm, k), lambda i, j: (i, 0)),
                  pl.BlockSpec((k, tn), lambda i, j: (0, j))],
        out_specs=pl.BlockSpec((None, dil, tm // dil, tn), lambda i, j: (i // per, 0, i % per, j)),
        out_shape=jax.ShapeDtypeStruct((nb, dil, seq // dil, n), BF16),
        scratch_shapes=[pltpu.VMEM((tn // LANES, tm, LANES), F32)],
        compiler_params=_params("parallel", "arbitrary"),
        name=name,
    )(x, w)


def _matmul_sigmoid(x, w, b, name):
    m, k = x.shape
    n = w.shape[1]
    tm, tn = _mm_tiles(m, n)
    return pl.pallas_call(
        _mm_sigmoid_kernel,
        grid=(m // tm, n // tn),
        in_specs=[pl.BlockSpec((tm, k), lambda i, j: (i, 0)),
                  pl.BlockSpec((k, tn), lambda i, j: (0, j)),
                  pl.BlockSpec((1, tn), lambda i, j: (0, j))],
        out_specs=pl.BlockSpec((tm, tn), lambda i, j: (i, j)),
        out_shape=jax.ShapeDtypeStruct((m, n), BF16),
        compiler_params=_params("parallel", "arbitrary"),
        name=name,
    )(x, w, b.reshape(1, n))


def _matmul_residual(x, w, resid, gate, seq, name):
    m, k = x.shape
    n = w.shape[1]
    tm, tn = _mm_tiles(m, n)
    assert seq % tm == 0
    per = seq // tm
    return pl.pallas_call(
        _mm_resid_kernel,
        grid=(m // tm, n // tn),
        in_specs=[pl.BlockSpec((tm, k), lambda i, j: (i, 0)),
                  pl.BlockSpec((k, tn), lambda i, j: (0, j)),
                  pl.BlockSpec((tm, tn), lambda i, j: (i, j)),
                  pl.BlockSpec((None, 1, tn), lambda i, j: (i // per, 0, j))],
        out_specs=pl.BlockSpec((tm, tn), lambda i, j: (i, j)),
        out_shape=jax.ShapeDtypeStruct((m, n), F32),
        compiler_params=_params("parallel", "arbitrary"),
        name=name,
    )(x, w, resid, gate)


def _attn_a_kernel(q_ref, k_ref, v_ref, gq_ref, gk_ref, lam_ref, sub_ref, slope_ref, o_ref, kn_ref,
                   *, lam_init, tq, seq):
    qi = pl.program_id(2)

    @pl.when(qi == 0)
    def _():
        kn_ref[...] = _head_norm(k_ref[...].astype(F32), gk_ref[...]).astype(BF16)

    lv = lam_ref[...]
    s01 = jnp.sum(lv[0:1] * lv[1:2], axis=-1, keepdims=True)
    s23 = jnp.sum(lv[2:3] * lv[3:4], axis=-1, keepdims=True)
    lam = jnp.exp(s01) - jnp.exp(s23) + lam_init

    qn = _head_norm(q_ref[...].astype(F32), gq_ref[...])
    q2 = _split_heads(qn)
    sc = _qk(q2, kn_ref[...])
    row = qi * tq + lax.broadcasted_iota(I32, (tq, seq), 0)
    col = lax.broadcasted_iota(I32, (tq, seq), 1)
    bias = slope_ref[...] * jnp.abs(row - col).astype(F32)
    v = v_ref[...]

    def component(s):
        s = s - bias
        m = jnp.max(s, axis=-1, keepdims=True)
        p = jnp.exp(s - m)
        l = jnp.sum(p, axis=-1, keepdims=True)
        return jnp.dot(p.astype(BF16), v, preferred_element_type=F32) / l

    o = component(sc[:tq]) - lam * component(sc[tq:])
    ms = jnp.mean(o * o, axis=-1, keepdims=True)
    o_ref[...] = (o * lax.rsqrt(ms + RMS_EPS) * sub_ref[...] * (1.0 - lam_init)).astype(o_ref.dtype)


def _attn_a(qkv, gq, gk, a_lambda, a_subln, lam_init):
    nb, seq, _ = qkv.shape
    tq = 256
    per = SEG_W // LANES
    slopes = jnp.asarray(_alibi_slopes(A_HEADS)).reshape(A_HEADS, 1, 1)
    kern = functools.partial(_attn_a_kernel, lam_init=lam_init, tq=tq, seq=seq)
    return pl.pallas_call(
        kern,
        grid=(nb, A_HEADS, seq // tq),
        in_specs=[pl.BlockSpec((None, tq, LANES), lambda b, h, i: (b, i, SEG_AQ * per + h)),
                  pl.BlockSpec((None, seq, LANES), lambda b, h, i: (b, 0, SEG_AK * per + h)),
                  pl.BlockSpec((None, seq, LANES), lambda b, h, i: (b, 0, SEG_AV * per + h)),
                  pl.BlockSpec((1, LANES), lambda b, h, i: (0, 0)),
                  pl.BlockSpec((1, LANES), lambda b, h, i: (0, 0)),
                  pl.BlockSpec((4, HEAD_DIM), lambda b, h, i: (0, 0)),
                  pl.BlockSpec((1, LANES), lambda b, h, i: (0, 0)),
                  pl.BlockSpec((None, 1, 1), lambda b, h, i: (h, 0, 0))],
        out_specs=pl.BlockSpec((None, tq, LANES), lambda b, h, i: (b, i, h)),
        out_shape=jax.ShapeDtypeStruct((nb, seq, BRANCH_W), BF16),
        scratch_shapes=[pltpu.VMEM((seq, LANES), BF16)],
        compiler_params=_params("parallel", "arbitrary", "arbitrary"),
        name="attn_a",
    )(qkv, qkv, qkv, gq, gk, a_lambda, a_subln.reshape(1, LANES), slopes)


def _attn_b_kernel(q_ref, k_ref, v_ref, gq_ref, gk_ref, slope_ref, sink_ref, o_ref, k2_ref, v2_ref,
                   *, tq, nk, seq):
    j = pl.program_id(1)
    qi = pl.program_id(2)

    @pl.when(qi == 0)
    def _():
        half = (lax.broadcasted_iota(I32, (seq, LANES), 1) >= HEAD_DIM).astype(I32)
        own = half == j
        kn = _head_norm(k_ref[...].astype(F32), gk_ref[...])
        k2_ref[...] = jnp.where(own, kn, pltpu.roll(kn, HEAD_DIM, axis=1)).astype(BF16)
        v = v_ref[...].astype(F32)
        v2_ref[...] = jnp.where(own, v, pltpu.roll(v, HEAD_DIM, axis=1)).astype(BF16)

    ws = pl.multiple_of(jnp.clip(qi * tq - B_WINDOW, 0, seq - nk), LANES)
    kw = k2_ref[pl.ds(ws, nk), :]
    vw = v2_ref[pl.ds(ws, nk), :]
    q = q_ref[...].astype(F32)
    g = gq_ref[...]
    qa = _head_norm(q[:, :LANES], g)
    qb = _head_norm(q[:, LANES:], g)
    q4 = jnp.concatenate([_split_heads(qa), _split_heads(qb)], axis=0)
    sc = _qk(q4, kw)
    row = qi * tq + lax.broadcasted_iota(I32, (tq, nk), 0)
    col = ws + lax.broadcasted_iota(I32, (tq, nk), 1)
    rel = jnp.abs(row - col)
    valid = rel <= B_WINDOW
    relf = rel.astype(F32)
    outs = []
    for gi in range(B_HEADS // B_KV_HEADS):
        s = jnp.where(valid, sc[gi * tq:(gi + 1) * tq] - slope_ref[gi] * relf, NEG_INF)
        m = jnp.max(s, axis=-1, keepdims=True)
        p = jnp.exp(s - m)
        l = jnp.sum(p, axis=-1, keepdims=True) + jnp.exp(sink_ref[gi] - m)
        outs.append(jnp.dot(p.astype(BF16), vw, preferred_element_type=F32) / l)
    lo = _lo_lanes((tq, LANES))
    o = jnp.concatenate([jnp.where(lo, outs[0], outs[1]), jnp.where(lo, outs[2], outs[3])], axis=1)
    o_ref[...] = o.astype(o_ref.dtype)


def _attn_b(qkv, kvb, gq, gk, b_sink):
    nb, seq, _ = qkv.shape
    tq = 128
    nk = tq + 2 * B_WINDOW
    grp = B_HEADS // B_KV_HEADS
    qw = grp * HEAD_DIM
    slopes = jnp.asarray(_alibi_slopes(B_HEADS)).reshape(B_HEADS, 1, 1)
    kern = functools.partial(_attn_b_kernel, tq=tq, nk=nk, seq=seq)
    return pl.pallas_call(
        kern,
        grid=(nb, B_KV_HEADS, seq // tq),
        in_specs=[pl.BlockSpec((None, tq, qw), lambda b, j, i: (b, i, SEG_BQ * (SEG_W // qw) + j)),
                  pl.BlockSpec((None, seq, LANES), lambda b, j, i: (b, 0, 0)),
                  pl.BlockSpec((None, seq, LANES), lambda b, j, i: (b, 0, 1)),
                  pl.BlockSpec((1, LANES), lambda b, j, i: (0, 0)),
                  pl.BlockSpec((1, LANES), lambda b, j, i: (0, 0)),
                  pl.BlockSpec((grp, 1, 1), lambda b, j, i: (j, 0, 0)),
                  pl.BlockSpec((grp, 1, 1), lambda b, j, i: (j, 0, 0))],
        out_specs=pl.BlockSpec((None, tq, qw), lambda b, j, i: (b, i, j)),
        out_shape=jax.ShapeDtypeStruct((nb, seq, BRANCH_W), BF16),
        scratch_shapes=[pltpu.VMEM((seq, LANES), BF16), pltpu.VMEM((seq, LANES), BF16)],
        compiler_params=_params("parallel", "arbitrary", "arbitrary"),
        name="attn_b",
    )(qkv, kvb, kvb, gq, gk, slopes, b_sink.reshape(B_HEADS, 1, 1))


def _attn_c_kernel(q_ref, k_ref, v_ref, gq_ref, gk_ref, slope_ref, o_ref, lse_ref, kn_ref,
                   *, tq, nk, sub, side, dil):
    qi = pl.program_id(2)
    n_pairs = SEG_W // LANES

    @pl.when(qi == 0)
    def _():
        for hp in range(n_pairs):
            cs = slice(hp * LANES, (hp + 1) * LANES)
            kn_ref[:, cs] = _head_norm(k_ref[:, cs].astype(F32), gk_ref[...]).astype(BF16)

    ws = pl.multiple_of(jnp.clip(qi * tq - side, 0, sub - nk), HEAD_DIM)
    row = qi * tq + lax.broadcasted_iota(I32, (tq, nk), 0)
    col = ws + lax.broadcasted_iota(I32, (tq, nk), 1)
    rel = jnp.abs(row - col)
    valid = rel <= side
    relf = (dil * rel).astype(F32)
    lo = _lo_lanes((tq, LANES))
    for hp in range(n_pairs):
        cs = slice(hp * LANES, (hp + 1) * LANES)
        qn = _head_norm(q_ref[:, cs].astype(F32), gq_ref[...])
        sc = _qk(_split_heads(qn), kn_ref[pl.ds(ws, nk), cs])
        vw = v_ref[pl.ds(ws, nk), cs]
        o2, l2 = [], []
        for t in range(2):
            s = jnp.where(valid, sc[t * tq:(t + 1) * tq] - slope_ref[2 * hp + t] * relf, NEG_INF)
            m = jnp.max(s, axis=-1, keepdims=True)
            p = jnp.exp(s - m)
            l = jnp.sum(p, axis=-1, keepdims=True)
            o2.append(jnp.dot(p.astype(BF16), vw, preferred_element_type=F32) / l)
            l2.append(m + jnp.log(l))
        o_ref[:, cs] = jnp.where(lo, o2[0], o2[1]).astype(o_ref.dtype)
        lse_ref[:, cs] = jnp.where(lo, l2[0], l2[1])


def _attn_c_group(src, seg, gi, gq, gk):
    nb, dil, sub, _ = src.shape
    win, gdil = C_DILATIONS[gi]
    assert gdil == dil
    side = win // (2 * dil)
    tq = min(256, sub)
    nk = min(tq + 2 * side, sub)
    slopes = jnp.asarray(_alibi_slopes(N_DIL * C_HEADS).reshape(N_DIL, C_HEADS)[gi]).reshape(C_HEADS, 1, 1)
    kern = functools.partial(_attn_c_kernel, tq=tq, nk=nk, sub=sub, side=side, dil=dil)
    return pl.pallas_call(
        kern,
        grid=(nb, dil, sub // tq),
        in_specs=[pl.BlockSpec((None, None, tq, SEG_W), lambda b, r, i: (b, r, i, seg)),
                  pl.BlockSpec((None, None, sub, SEG_W), lambda b, r, i: (b, r, 0, seg + 1)),
                  pl.BlockSpec((None, None, sub, SEG_W), lambda b, r, i: (b, r, 0, seg + 2)),
                  pl.BlockSpec((1, LANES), lambda b, r, i: (0, 0)),
                  pl.BlockSpec((1, LANES), lambda b, r, i: (0, 0)),
                  pl.BlockSpec((C_HEADS, 1, 1), lambda b, r, i: (0, 0, 0))],
        out_specs=[pl.BlockSpec((None, None, tq, SEG_W), lambda b, r, i: (b, r, i, 0)),
                   pl.BlockSpec((None, None, tq, SEG_W), lambda b, r, i: (b, r, i, 0))],
        out_shape=[jax.ShapeDtypeStruct((nb, dil, sub, SEG_W), BF16),
                   jax.ShapeDtypeStruct((nb, dil, sub, SEG_W), F32)],
        scratch_shapes=[pltpu.VMEM((sub, SEG_W), BF16)],
        compiler_params=_params("parallel", "arbitrary", "arbitrary"),
        name=f"attn_c{gi}",
    )(src, src, src, gq, gk, slopes)


def _attn_d_kernel(q_ref, k_ref, v_ref, gq_ref, gk_ref, bias_ref, o_ref, kn_ref, *, rows, nk, rb):
    blk = pl.program_id(1)
    n_pairs = SEG_W // LANES

    @pl.when(blk == 0)
    def _():
        for hp in range(n_pairs):
            cs = slice(hp * LANES, (hp + 1) * LANES)
            kn_ref[:, cs] = _head_norm(k_ref[:, cs].astype(F32), gk_ref[...]).astype(BF16)

    kr = nk // GRID_W
    lo = _lo_lanes((GRID_W, LANES))
    for rr in range(rb):
        r = blk * rb + rr
        kr0 = jnp.clip(r - kr // 2, 0, rows - kr)
        d0 = kr0 - r + NA_ROWS - 1
        ws = pl.multiple_of(kr0 * GRID_W, GRID_W)
        rs = slice(rr * GRID_W, (rr + 1) * GRID_W)
        for hp in range(n_pairs):
            cs = slice(hp * LANES, (hp + 1) * LANES)
            qn = _head_norm(q_ref[rs, cs].astype(F32), gq_ref[...])
            sc = _qk(_split_heads(qn), kn_ref[pl.ds(ws, nk), cs])
            vw = v_ref[pl.ds(ws, nk), cs]
            o2 = []
            for t in range(2):
                s = sc[t * GRID_W:(t + 1) * GRID_W] + bias_ref[d0, 2 * hp + t]
                m = jnp.max(s, axis=-1, keepdims=True)
                p = jnp.exp(s - m)
                l = jnp.sum(p, axis=-1, keepdims=True)
                o2.append(jnp.dot(p.astype(BF16), vw, preferred_element_type=F32) / l)
            o_ref[rs, cs] = jnp.where(lo, o2[0], o2[1]).astype(o_ref.dtype)


def _na_bias_table(rpb, rows):
    kr = min(NA_ROWS, rows)
    qc = np.arange(GRID_W)[:, None]
    kc = np.arange(GRID_W)[None, :]
    col_start = np.clip(qc - NA_COLS // 2, 0, GRID_W - NA_COLS)
    ok = (kc >= col_start) & (kc < col_start + NA_COLS)
    dc = np.clip(kc - qc + NA_COLS - 1, 0, 2 * NA_COLS - 2)
    drow = np.arange(kr)[:, None] + np.arange(kr)[None, :]
    t = rpb[:, drow][:, :, :, dc]
    t = jnp.where(ok[None, None, None], t.astype(F32), NEG_INF)
    t = t.transpose(1, 0, 3, 2, 4)
    return t.reshape(kr, D_HEADS, GRID_W, kr * GRID_W)


def _attn_d(qkv, gq, gk, d_rpb):
    nb, seq, _ = qkv.shape
    rows = seq // GRID_W
    kr = min(NA_ROWS, rows)
    nk = kr * GRID_W
    table = _na_bias_table(d_rpb, rows)
    rb = 4
    assert rows % rb == 0
    kern = functools.partial(_attn_d_kernel, rows=rows, nk=nk, rb=rb)
    return pl.pallas_call(
        kern,
        grid=(nb, rows // rb),
        in_specs=[pl.BlockSpec((None, rb * GRID_W, SEG_W), lambda b, r: (b, r, SEG_DQ)),
                  pl.BlockSpec((None, seq, SEG_W), lambda b, r: (b, 0, SEG_DK)),
                  pl.BlockSpec((None, seq, SEG_W), lambda b, r: (b, 0, SEG_DV)),
                  pl.BlockSpec((1, LANES), lambda b, r: (0, 0)),
                  pl.BlockSpec((1, LANES), lambda b, r: (0, 0)),
                  pl.BlockSpec((kr, D_HEADS, GRID_W, nk), lambda b, r: (0, 0, 0, 0))],
        out_specs=pl.BlockSpec((None, rb * GRID_W, SEG_W), lambda b, r: (b, r, 0)),
        out_shape=jax.ShapeDtypeStruct((nb, seq, BRANCH_W), BF16),
        scratch_shapes=[pltpu.VMEM((seq, SEG_W), BF16)],
        compiler_params=_params("parallel", "arbitrary"),
        name="attn_d",
    )(qkv, qkv, qkv, gq, gk, table)


def _to_token_order(blk_ref, scr_ref):
    dil, n, _ = blk_ref.shape
    for r in range(dil):
        for c in range(SEG_W // LANES):
            scr_ref[c, pl.ds(r, n, stride=dil), :] = blk_ref[r, :, c * LANES:(c + 1) * LANES].astype(F32)


def _merge_kernel(oa_ref, ob_ref, oc0_ref, oc1_ref, oc2_ref, l0_ref, l1_ref, l2_ref, od_ref,
                  gates_ref, wb_ref, o_ref, s_o1, s_l1, s_o2, s_l2):
    _to_token_order(oc1_ref, s_o1)
    _to_token_order(l1_ref, s_l1)
    _to_token_order(oc2_ref, s_o2)
    _to_token_order(l2_ref, s_l2)
    slabs = []
    for c in range(SEG_W // LANES):
        cs = slice(c * LANES, (c + 1) * LANES)
        l0, l1, l2 = l0_ref[:, cs], s_l1[c], s_l2[c]
        lm = jnp.maximum(jnp.maximum(l0, l1), l2)
        e0, e1, e2 = jnp.exp(l0 - lm), jnp.exp(l1 - lm), jnp.exp(l2 - lm)
        oc = (e0 * oc0_ref[:, cs].astype(F32) + e1 * s_o1[c] + e2 * s_o2[c]) / (e0 + e1 + e2)
        slabs.append(oc.astype(BF16))
    branches = (oa_ref[...], ob_ref[...], jnp.concatenate(slabs, axis=1), od_ref[...])
    acc = None
    for n, br in enumerate(branches):
        proj = jnp.dot(br, wb_ref[n], preferred_element_type=F32)
        term = gates_ref[:, n * D_MODEL:(n + 1) * D_MODEL].astype(F32) * proj
        acc = term if acc is None else acc + term
    o_ref[...] = acc.astype(o_ref.dtype)


def _merge(oa, ob, oc, lses, od, gates, w_branch):
    nb, seq, _ = oa.shape
    tm = 256
    per = seq // tm
    tok = pl.BlockSpec((None, tm, BRANCH_W), lambda b, i: (b, i, 0))

    def res(dil):
        return pl.BlockSpec((None, dil, tm // dil, BRANCH_W), lambda b, i: (b, 0, i, 0))

    dils = [d for _, d in C_DILATIONS]
    assert dils[0] == 1
    res0 = pl.BlockSpec((None, None, tm, BRANCH_W), lambda b, i: (b, 0, i, 0))
    slab = pltpu.VMEM((BRANCH_W // LANES, tm, LANES), F32)
    return pl.pallas_call(
        _merge_kernel,
        grid=(nb, per),
        in_specs=[tok, tok, res0, res(dils[1]), res(dils[2]), res0, res(dils[1]), res(dils[2]), tok,
                  pl.BlockSpec((tm, N_BRANCH * D_MODEL), lambda b, i: (b * per + i, 0)),
                  pl.BlockSpec((N_BRANCH, BRANCH_W, D_MODEL), lambda b, i: (0, 0, 0))],
        out_specs=pl.BlockSpec((tm, D_MODEL), lambda b, i: (b * per + i, 0)),
        out_shape=jax.ShapeDtypeStruct((nb * seq, D_MODEL), BF16),
        scratch_shapes=[slab, slab, slab, slab],
        compiler_params=_params("parallel", "arbitrary"),
        name="merge",
    )(oa, ob, oc[0], oc[1], oc[2], lses[0], lses[1], lses[2], od, gates, w_branch)


def _route_kernel(x_ref, g_ref, sc_ref, sh_ref, wr_ref, br_ref, hf_ref, idx_ref, p_ref):
    h = _modulated_norm(x_ref[...], g_ref, sc_ref, sh_ref)
    _store_row_tiles(hf_ref, h)
    logits = jnp.dot(h.astype(BF16), wr_ref[...], preferred_element_type=F32) + br_ref[...]
    lane = lax.broadcasted_iota(I32, logits.shape, 1)
    idx_t = jnp.zeros(logits.shape, I32)
    e_t = jnp.zeros(logits.shape, F32)
    den = None
    top = None
    for k in range(TOP_K):
        m = jnp.max(logits, axis=-1, keepdims=True)
        idx = jnp.min(jnp.where(logits == m, lane, LANES), axis=-1, keepdims=True)
        logits = jnp.where(lane == idx, -jnp.inf, logits)
        if k == 0:
            top = m
        e = jnp.exp(m - top)
        den = e if den is None else den + e
        idx_t = jnp.where(lane == k, idx, idx_t)
        e_t = jnp.where(lane == k, e, e_t)
    idx_ref[...] = idx_t
    p_ref[...] = e_t / den


def _route(x, g, sc, sh, w_router, b_router):
    nb, s, d = x.shape
    tm = 256
    wr = jnp.zeros((d, LANES), BF16).at[:, :N_EXPERTS].set(w_router.astype(BF16))
    br = jnp.full((1, LANES), NEG_INF, F32).at[0, :N_EXPERTS].set(b_router)
    per = s // tm
    return pl.pallas_call(
        _route_kernel,
        grid=(nb, per),
        in_specs=[pl.BlockSpec((None, tm, d), lambda b, i: (b, i, 0)),
                  pl.BlockSpec((1, d), lambda b, i: (0, 0)),
                  pl.BlockSpec((None, 1, d), lambda b, i: (b, 0, 0)),
                  pl.BlockSpec((None, 1, d), lambda b, i: (b, 0, 0)),
                  pl.BlockSpec((d, LANES), lambda b, i: (0, 0)),
                  pl.BlockSpec((1, LANES), lambda b, i: (0, 0))],
        out_specs=[pl.BlockSpec((tm * ROW_TILE, LANES), lambda b, i: (b * per + i, 0)),
                   pl.BlockSpec((tm, LANES), lambda b, i: (b * per + i, 0)),
                   pl.BlockSpec((tm, LANES), lambda b, i: (b * per + i, 0))],
        out_shape=[jax.ShapeDtypeStruct((nb * s * ROW_TILE, LANES), U32),
                   jax.ShapeDtypeStruct((nb * s, LANES), I32),
                   jax.ShapeDtypeStruct((nb * s, LANES), F32)],
        compiler_params=_params("parallel", "parallel"),
        name="route",
    )(x, g.reshape(1, d), sc, sh, wr, br)


def _tile_rows(t):
    return pl.ds(pl.multiple_of(t * ROW_TILE, ROW_TILE), ROW_TILE)


def _dispatch_kernel(dest_ref, src_ref, init_ref, o_ref, sem, *, rows):
    del init_ref

    def row_copy(t, d):
        return pltpu.make_async_copy(src_ref.at[_tile_rows(t), :], o_ref.at[_tile_rows(d), :], sem)

    def issue(t, c):
        for k in range(TOP_K):
            row_copy(t, dest_ref[0, 0, t * TOP_K + k]).start(priority=k % DMA_QUEUES)
        return c
    lax.fori_loop(0, rows, issue, 0, unroll=DMA_UNROLL // TOP_K)

    def drain(t, c):
        for k in range(TOP_K):
            row_copy(t, 0).wait()
        return c
    lax.fori_loop(0, rows, drain, 0, unroll=DMA_UNROLL // TOP_K)


def _dispatch_rows(src, dest, n_slots):
    rows = DISPATCH_ROWS
    n_tok = src.shape[0] // ROW_TILE
    nblk = n_tok // rows
    kern = functools.partial(_dispatch_kernel, rows=rows)
    return pl.pallas_call(
        kern,
        grid=(nblk,),
        in_specs=[pl.BlockSpec((1, 1, rows * TOP_K), lambda i: (i, 0, 0), memory_space=pltpu.SMEM),
                  pl.BlockSpec((rows * ROW_TILE, LANES), lambda i: (i, 0)),
                  pl.BlockSpec(memory_space=pl.ANY)],
        out_specs=pl.BlockSpec(memory_space=pl.ANY),
        out_shape=jax.ShapeDtypeStruct((n_slots * ROW_TILE, LANES), U32),
        scratch_shapes=[pltpu.SemaphoreType.DMA(())],
        input_output_aliases={2: 0},
        compiler_params=_params("arbitrary"),
        name="moe_dispatch",
    )(dest.reshape(nblk, 1, rows * TOP_K), src, jnp.zeros((n_slots * ROW_TILE, LANES), U32))


def _expert_kernel(be_ref, nu_ref, xs_ref, wg_ref, wu_ref, wdl_ref, wdh_ref, bg_ref, bu_ref, bdl_ref, bdh_ref,
                   o_ref, xb_ref, a_ref, *, n_up, n_down):
    i = pl.program_id(0)
    f = pl.program_id(1)
    used = i < nu_ref[0]
    rows = xb_ref.shape[0]
    tf = wg_ref.shape[1]

    @pl.when(used & (f == 0))
    def _():
        for c in range(ROW_TILE):
            lo, hi = _load_row_tile_chunk(xs_ref, c)
            xb_ref[:, c * LANES:(c + 1) * LANES] = lo.astype(BF16)
            xb_ref[:, HALF_D + c * LANES:HALF_D + (c + 1) * LANES] = hi.astype(BF16)

    @pl.when(used & (f < n_up))
    def _():
        x = xb_ref[...]
        g = jnp.dot(x, wg_ref[...].astype(BF16), preferred_element_type=F32) + bg_ref[...]
        u = jnp.dot(x, wu_ref[...].astype(BF16), preferred_element_type=F32) + bu_ref[...]
        g = jnp.minimum(g, SWIGLU_LIMIT)
        u = jnp.clip(u, -SWIGLU_LIMIT, SWIGLU_LIMIT)
        a = ((u + 1.0) * g * jax.nn.sigmoid(SWIGLU_ALPHA * g)).astype(BF16)
        for ff in range(n_up):
            @pl.when(f == ff)
            def _(ff=ff):
                a_ref[:, ff * tf:(ff + 1) * tf] = a

    @pl.when(used & (f >= n_up))
    def _():
        m = f - n_up
        a = a_ref[...]
        lo = jnp.dot(a, wdl_ref[...].astype(BF16), preferred_element_type=F32) + bdl_ref[...]
        hi = jnp.dot(a, wdh_ref[...].astype(BF16), preferred_element_type=F32) + bdh_ref[...]
        packed = _pack_pair(lo, hi)
        for j in range(packed.shape[1] // LANES):
            c = m * (packed.shape[1] // LANES) + j
            o_ref[pl.ds(c, rows, stride=ROW_TILE), :] = packed[:, j * LANES:(j + 1) * LANES]

    @pl.when(jnp.logical_not(used) & (f == n_up + n_down - 1))
    def _():
        o_ref[...] = jnp.zeros(o_ref.shape, o_ref.dtype)


def _experts(xs, block_expert, n_used, layer, w_gate_up, b_gate_up, w_down, b_down):
    rows, tf, tn = MOE_ROWS, MOE_TF, MOE_TN
    n_slots = xs.shape[0] // ROW_TILE
    nblk = n_slots // rows
    n_up = D_FF // tf
    n_down = HALF_D // tn
    depth, e = w_gate_up.shape[:2]
    kern = functools.partial(_expert_kernel, n_up=n_up, n_down=n_down)

    def up(i, f, nu):
        return jnp.where(i < nu[0], jnp.minimum(f, n_up - 1), n_up - 1)

    def down(i, f, nu):
        return jnp.where(i < nu[0], jnp.maximum(f - n_up, 0), n_down - 1)

    return pl.pallas_call(
        kern,
        grid_spec=pltpu.PrefetchScalarGridSpec(
            num_scalar_prefetch=2,
            grid=(nblk, n_up + n_down),
            in_specs=[pl.BlockSpec((rows * ROW_TILE, LANES), lambda i, f, be, nu: (i, 0)),
                      pl.BlockSpec((None, None, D_MODEL, tf), lambda i, f, be, nu: (layer, be[i], 0, up(i, f, nu))),
                      pl.BlockSpec((None, None, D_MODEL, tf),
                                   lambda i, f, be, nu: (layer, be[i], 0, n_up + up(i, f, nu))),
                      pl.BlockSpec((None, None, D_FF, tn), lambda i, f, be, nu: (layer, be[i], 0, down(i, f, nu))),
                      pl.BlockSpec((None, None, D_FF, tn),
                                   lambda i, f, be, nu: (layer, be[i], 0, n_down + down(i, f, nu))),
                      pl.BlockSpec((None, None, 1, tf), lambda i, f, be, nu: (layer, be[i], 0, up(i, f, nu))),
                      pl.BlockSpec((None, None, 1, tf),
                                   lambda i, f, be, nu: (layer, be[i], 0, n_up + up(i, f, nu))),
                      pl.BlockSpec((None, None, 1, tn), lambda i, f, be, nu: (layer, be[i], 0, down(i, f, nu))),
                      pl.BlockSpec((None, None, 1, tn),
                                   lambda i, f, be, nu: (layer, be[i], 0, n_down + down(i, f, nu)))],
            out_specs=pl.BlockSpec((rows * ROW_TILE, LANES), lambda i, f, be, nu: (i, 0)),
            scratch_shapes=[pltpu.VMEM((rows, D_MODEL), BF16), pltpu.VMEM((rows, D_FF), BF16)],
        ),
        out_shape=jax.ShapeDtypeStruct((n_slots * ROW_TILE, LANES), U32),
        compiler_params=_params("arbitrary", "arbitrary"),
        name="moe_experts",
    )(block_expert, n_used, xs, w_gate_up, w_gate_up, w_down, w_down,
      b_gate_up.reshape(depth, e, 1, 2 * D_FF), b_gate_up.reshape(depth, e, 1, 2 * D_FF),
      b_down.reshape(depth, e, 1, D_MODEL), b_down.reshape(depth, e, 1, D_MODEL))


def _combine_kernel(dest_ref, src_ref, p_ref, x_ref, g_ref, o_ref, buf_ref, sem, *, rows):
    def row_copy(k, t, d):
        return pltpu.make_async_copy(src_ref.at[_tile_rows(d), :], buf_ref.at[k, _tile_rows(t), :], sem)

    for k in range(TOP_K):
        def issue(t2, c, k=k):
            for par in range(DMA_QUEUES):
                t = t2 * DMA_QUEUES + par
                row_copy(k, t, dest_ref[0, 0, t * TOP_K + k]).start(priority=par)
            return c
        lax.fori_loop(0, rows // DMA_QUEUES, issue, 0, unroll=DMA_UNROLL // DMA_QUEUES)
    for k in range(TOP_K):
        def drain(t, c, k=k):
            row_copy(k, t, 0).wait()
            return c
        lax.fori_loop(0, rows, drain, 0, unroll=DMA_UNROLL)

    p = p_ref[...]
    pk = [p[:, k:k + 1] for k in range(TOP_K)]
    for c in range(ROW_TILE):
        y_lo = None
        y_hi = None
        for k in range(TOP_K):
            lo, hi = _load_row_tile_chunk(buf_ref.at[k], c)
            y_lo = pk[k] * lo if y_lo is None else y_lo + pk[k] * lo
            y_hi = pk[k] * hi if y_hi is None else y_hi + pk[k] * hi
        cl = slice(c * LANES, (c + 1) * LANES)
        ch = slice(HALF_D + c * LANES, HALF_D + (c + 1) * LANES)
        o_ref[:, cl] = x_ref[:, cl] + g_ref[:, cl] * y_lo
        o_ref[:, ch] = x_ref[:, ch] + g_ref[:, ch] * y_hi


def _combine(outs, dest, probs, x, gate, seq):
    t, d = x.shape
    rows = COMBINE_ROWS
    nblk = t // rows
    per = seq // rows
    kern = functools.partial(_combine_kernel, rows=rows)
    return pl.pallas_call(
        kern,
        grid=(nblk,),
        in_specs=[pl.BlockSpec((1, 1, rows * TOP_K), lambda i: (i, 0, 0), memory_space=pltpu.SMEM),
                  pl.BlockSpec(memory_space=pl.ANY),
                  pl.BlockSpec((rows, LANES), lambda i: (i, 0)),
                  pl.BlockSpec((rows, d), lambda i: (i, 0)),
                  pl.BlockSpec((None, 1, d), lambda i: (i // per, 0, 0))],
        out_specs=pl.BlockSpec((rows, d), lambda i: (i, 0)),
        out_shape=jax.ShapeDtypeStruct((t, d), F32),
        scratch_shapes=[pltpu.VMEM((TOP_K, rows * ROW_TILE, LANES), U32), pltpu.SemaphoreType.DMA(())],
        compiler_params=_params("arbitrary"),
        name="moe_combine",
    )(dest.reshape(nblk, 1, rows * TOP_K), outs, probs, x, gate)


def _routing_plan(top_idx, n_tok):
    expert = top_idx.reshape(-1)
    onehot = jax.nn.one_hot(expert, N_EXPERTS, dtype=I32)
    csum = jnp.cumsum(onehot, axis=0)
    rank = jnp.take_along_axis(csum, expert[:, None], axis=1)[:, 0] - 1
    padded = (csum[-1] + MOE_ROWS - 1) // MOE_ROWS * MOE_ROWS
    pad_end = jnp.cumsum(padded)
    dest = (pad_end[expert] - padded[expert] + rank).astype(I32)
    n_blocks = -(-(n_tok * TOP_K) // MOE_ROWS) + N_EXPERTS
    block_expert = jnp.minimum(
        jnp.searchsorted(pad_end, jnp.arange(n_blocks, dtype=I32) * MOE_ROWS, side='right'),
        N_EXPERTS - 1).astype(I32)
    n_blocks_used = (pad_end[-1:] // MOE_ROWS).astype(I32)
    return dest, block_expert, n_blocks_used, n_blocks * MOE_ROWS


def _moe(x, g, sc, sh, gate, layer, w_router, b_router, w_gate_up, b_gate_up, w_down, b_down):
    nb, seq, d = x.shape
    n_tok = nb * seq
    hf, idx_t, p_t = _route(x, g, sc, sh, w_router, b_router)
    dest, block_expert, n_blocks_used, n_slots = _routing_plan(idx_t[:, :TOP_K], n_tok)
    xs = _dispatch_rows(hf, dest, n_slots)
    outs = _experts(xs, block_expert, n_blocks_used, layer, w_gate_up, b_gate_up, w_down, b_down)
    y = _combine(outs, dest, p_t, x.reshape(n_tok, d), gate, seq)
    return y.reshape(nb, seq, d)


def _pair_gain(g, scale=1.0):
    return (jnp.tile(g.astype(F32), LANES // HEAD_DIM) * scale).reshape(1, LANES)


def _token_mixer(x, layer, lam_init, g, sc, sh, gate, p):
    nb, seq, d = x.shape
    n_tok = nb * seq
    hm = _normmod(x, g, sc, sh).reshape(n_tok, d)
    w_in = p['w_in'][layer]
    w_main = jnp.concatenate([w_in[:, :BKV_LO], w_in[:, C_LO:C_LO + C_GROUP_W], w_in[:, D_LO:]],
                             axis=1).astype(BF16)
    qkv = _matmul(hm, w_main, "in_proj").reshape(nb, seq, QKV_W)
    kvb = _matmul(hm, w_in[:, BKV_LO:BKV_HI].astype(BF16), "in_proj_bkv").reshape(nb, seq, BKV_HI - BKV_LO)
    c_src = [(qkv.reshape(nb, 1, seq, QKV_W), SEG_C0)]
    for gi in range(1, N_DIL):
        lo = C_LO + gi * C_GROUP_W
        c_src.append((_matmul_by_residue(hm, w_in[:, lo:lo + C_GROUP_W].astype(BF16), nb, seq,
                                         C_DILATIONS[gi][1], f"in_proj_c{gi}"), 0))
    qs = HEAD_DIM ** -0.5
    oa = _attn_a(qkv, _pair_gain(p['a_qk_norm'][layer, 0], qs), _pair_gain(p['a_qk_norm'][layer, 1]),
                 p['a_lambda'][layer], p['a_subln'][layer], lam_init)
    ob = _attn_b(qkv, kvb, _pair_gain(p['b_qk_norm'][layer, 0], qs), _pair_gain(p['b_qk_norm'][layer, 1]),
                 p['b_sink'][layer])
    ocs, lses = [], []
    for gi in range(N_DIL):
        o, lse = _attn_c_group(c_src[gi][0], c_src[gi][1], gi, _pair_gain(p['c_qk_norm'][layer, gi, 0], qs),
                               _pair_gain(p['c_qk_norm'][layer, gi, 1]))
        ocs.append(o)
        lses.append(lse)
    od = _attn_d(qkv, _pair_gain(p['d_qk_norm'][layer, 0], qs), _pair_gain(p['d_qk_norm'][layer, 1]),
                 p['d_rpb'][layer])
    gates = _matmul_sigmoid(hm, p['w_gate'][layer].astype(BF16), p['b_gate'][layer], "gate_proj")
    mixed = _merge(oa, ob, ocs, lses, od, gates, p['w_branch'][layer].astype(BF16))
    y = _matmul_residual(mixed, p['w_out'][layer].astype(BF16), x.reshape(n_tok, d), gate, seq, "out_proj")
    return y.reshape(nb, seq, d)


def kernel(x_prompt, x_sample, c_prompt, c_sample, w_ada, b_ada, norm_g, w_in, a_qk_norm, a_lambda, a_subln, b_qk_norm, b_sink, c_qk_norm, d_qk_norm, d_rpb, w_gate, b_gate, w_branch, w_out, w_router, b_router, w_gate_up, b_gate_up, w_down, b_down):
    p = dict(w_in=w_in, a_qk_norm=a_qk_norm, a_lambda=a_lambda, a_subln=a_subln, b_qk_norm=b_qk_norm,
             b_sink=b_sink, c_qk_norm=c_qk_norm, d_qk_norm=d_qk_norm, d_rpb=d_rpb, w_gate=w_gate,
             b_gate=b_gate, w_branch=w_branch, w_out=w_out)
    assert x_prompt.shape[1:] == x_sample.shape[1:]
    n_prompt = x_prompt.shape[0]
    x = jnp.concatenate([x_prompt, x_sample], axis=0)
    c = jnp.concatenate([c_prompt, c_sample], axis=0)
    nb, seq, d = x.shape
    depth = w_ada.shape[0]
    mod = _ada_mod(c, w_ada, b_ada).reshape(depth, nb, 6, 1, d)
    for layer in range(depth):
        sh1, sc1, g1, sh2, sc2, g2 = (mod[layer, :, i] for i in range(6))
        lam_init = 0.8 - 0.6 * math.exp(-0.3 * layer)
        x = _token_mixer(x, layer, lam_init, norm_g[layer, 0], sc1, sh1, g1, p)
        x = _moe(x, norm_g[layer, 1], sc2, sh2, g2, layer, w_router[layer], b_router[layer],
                 w_gate_up, b_gate_up, w_down, b_down)
    return x[:n_prompt], x[n_prompt:]
```

```python
import functools
import math

import numpy as np
import jax
import jax.numpy as jnp
from jax import lax
from jax.experimental import pallas as pl
from jax.experimental.pallas import tpu as pltpu

F32 = jnp.float32
BF16 = jnp.bfloat16
U32 = jnp.uint32
I32 = jnp.int32

D_MODEL = 2048
DEPTH = 4
HEAD_DIM = 64
N_BRANCH = 4
BRANCH_W = D_MODEL // N_BRANCH
A_HEADS = 4
B_HEADS = 8
B_KV_HEADS = 2
B_WINDOW = 128
C_HEADS = 8
C_DILATIONS = ((128, 1), (512, 4), (2048, 16))
N_DIL = 3
GRID_W = 64
D_HEADS = 8
NA_ROWS = 8
NA_COLS = 16
N_EXPERTS = 32
TOP_K = 4
D_FF = D_MODEL
SWIGLU_ALPHA = 1.702
SWIGLU_LIMIT = 7.0
RMS_EPS = 1e-6
NEG_INF = -1e30

LANES = 128
HALF_D = D_MODEL // 2
SEG_W = 512
SEG_AQ, SEG_AK, SEG_AV, SEG_BQ = 0, 1, 2, 3
SEG_C0 = 4
SEG_DQ, SEG_DK, SEG_DV = 7, 8, 9
N_SEG = 10
QKV_W = N_SEG * SEG_W
BKV_LO, BKV_HI = 2048, 2304
C_LO = 2304
C_GROUP_W = 3 * SEG_W
D_LO = C_LO + N_DIL * C_GROUP_W
ROW_TILE = 8

VMEM_LIMIT = 56 * 1024 * 1024

MOE_ROWS = 1024
MOE_TF = 512
MOE_TN = 256
DISPATCH_ROWS = 256
COMBINE_ROWS = 256
DMA_UNROLL = 8
DMA_QUEUES = 2


def _params(*sem):
    return pltpu.CompilerParams(dimension_semantics=sem, vmem_limit_bytes=VMEM_LIMIT)


def _alibi_slopes(n):
    return (2.0 ** (-8.0 * np.arange(1, n + 1, dtype=np.float64) / n)).astype(np.float32)


def _group_mean_sq(x):
    r = lax.broadcasted_iota(I32, (LANES, LANES), 0) // HEAD_DIM
    c = lax.broadcasted_iota(I32, (LANES, LANES), 1) // HEAD_DIM
    ones = jnp.where(r == c, 1.0, 0.0).astype(BF16)
    return jnp.dot((x * x).astype(BF16), ones, preferred_element_type=F32) * (1.0 / HEAD_DIM)


def _head_norm(x, gain):
    return x * lax.rsqrt(_group_mean_sq(x) + RMS_EPS) * gain


def _lo_lanes(shape):
    return lax.broadcasted_iota(I32, shape, 1) < HEAD_DIM


def _split_heads(x):
    lo = _lo_lanes(x.shape)
    return jnp.concatenate([jnp.where(lo, x, 0.0), jnp.where(lo, 0.0, x)], axis=0).astype(BF16)


def _qk(q, k):
    return lax.dot_general(q, k, (((1,), (1,)), ((), ())), preferred_element_type=F32)


def _pack_pair(a, b):
    ua = lax.bitcast_convert_type(a.astype(BF16).astype(F32), U32)
    ub = lax.bitcast_convert_type(b.astype(BF16).astype(F32), U32)
    return (ua >> 16) | (ub & jnp.uint32(0xFFFF0000))


def _unpack_pair(w):
    a = lax.bitcast_convert_type(w << 16, F32)
    b = lax.bitcast_convert_type(w & jnp.uint32(0xFFFF0000), F32)
    return a, b


def _store_row_tiles(ref, val):
    n = val.shape[0]
    for c in range(ROW_TILE):
        lo = val[:, c * LANES:(c + 1) * LANES]
        hi = val[:, HALF_D + c * LANES:HALF_D + (c + 1) * LANES]
        ref[pl.ds(c, n, stride=ROW_TILE), :] = _pack_pair(lo, hi)


def _load_row_tile_chunk(ref, c):
    n = ref.shape[0] // ROW_TILE
    return _unpack_pair(ref[pl.ds(c, n, stride=ROW_TILE), :])


def _ada_kernel(c_ref, w_ref, b_ref, o_ref):
    c = c_ref[...]
    h = (c * jax.nn.sigmoid(c)).astype(BF16)
    o_ref[...] = jnp.dot(h, w_ref[...].astype(BF16), preferred_element_type=F32) + b_ref[...]


def _ada_mod(c, w_ada, b_ada):
    nb, d = c.shape
    depth, _, n = w_ada.shape
    tn = 1024
    return pl.pallas_call(
        _ada_kernel,
        grid=(depth, n // tn),
        in_specs=[pl.BlockSpec((nb, d), lambda l, j: (0, 0)),
                  pl.BlockSpec((None, d, tn), lambda l, j: (l, 0, j)),
                  pl.BlockSpec((None, 1, tn), lambda l, j: (l, 0, j))],
        out_specs=pl.BlockSpec((None, nb, tn), lambda l, j: (l, 0, j)),
        out_shape=jax.ShapeDtypeStruct((depth, nb, n), F32),
        compiler_params=_params("parallel", "parallel"),
        name="ada_mod",
    )(c, w_ada, b_ada.reshape(depth, 1, n))


def _modulated_norm(x, g_ref, sc_ref, sh_ref):
    ms = jnp.mean(x * x, axis=-1, keepdims=True)
    y = x * lax.rsqrt(ms + RMS_EPS) * g_ref[...]
    return y * (1.0 + sc_ref[...]) + sh_ref[...]


def _normmod_kernel(x_ref, g_ref, sc_ref, sh_ref, o_ref):
    o_ref[...] = _modulated_norm(x_ref[...], g_ref, sc_ref, sh_ref).astype(o_ref.dtype)


def _normmod(x, g, sc, sh):
    nb, s, d = x.shape
    tm = 512
    return pl.pallas_call(
        _normmod_kernel,
        grid=(nb, s // tm),
        in_specs=[pl.BlockSpec((None, tm, d), lambda b, i: (b, i, 0)),
                  pl.BlockSpec((1, d), lambda b, i: (0, 0)),
                  pl.BlockSpec((None, 1, d), lambda b, i: (b, 0, 0)),
                  pl.BlockSpec((None, 1, d), lambda b, i: (b, 0, 0))],
        out_specs=pl.BlockSpec((None, tm, d), lambda b, i: (b, i, 0)),
        out_shape=jax.ShapeDtypeStruct((nb, s, d), BF16),
        compiler_params=_params("parallel", "parallel"),
        name="normmod",
    )(x, g.reshape(1, d), sc, sh)


def _mm_kernel(x_ref, w_ref, o_ref):
    o_ref[...] = jnp.dot(x_ref[...], w_ref[...], preferred_element_type=F32).astype(o_ref.dtype)


def _mm_sigmoid_kernel(x_ref, w_ref, b_ref, o_ref):
    acc = jnp.dot(x_ref[...], w_ref[...], preferred_element_type=F32) + b_ref[...]
    o_ref[...] = jax.nn.sigmoid(acc).astype(o_ref.dtype)


def _mm_resid_kernel(x_ref, w_ref, r_ref, g_ref, o_ref):
    acc = jnp.dot(x_ref[...], w_ref[...], preferred_element_type=F32)
    o_ref[...] = r_ref[...] + g_ref[...] * acc


def _mm_tiles(m, n):
    tm = min(1024, m)
    tn = 1024 if n % 1024 == 0 else min(512, n)
    assert m % tm == 0 and n % tn == 0
    return tm, tn


def _matmul(x, w, name):
    m, k = x.shape
    n = w.shape[1]
    tm, tn = _mm_tiles(m, n)
    return pl.pallas_call(
        _mm_kernel,
        grid=(m // tm, n // tn),
        in_specs=[pl.BlockSpec((tm, k), lambda i, j: (i, 0)),
                  pl.BlockSpec((k, tn), lambda i, j: (0, j))],
        out_specs=pl.BlockSpec((tm, tn), lambda i, j: (i, j)),
        out_shape=jax.ShapeDtypeStruct((m, n), BF16),
        compiler_params=_params("parallel", "arbitrary"),
        name=name,
    )(x, w)


def _mm_residue_kernel(x_ref, w_ref, o_ref, acc_ref, *, dil):
    tm, tn = acc_ref.shape[1], acc_ref.shape[0] * LANES
    acc = jnp.dot(x_ref[...], w_ref[...], preferred_element_type=F32)
    for c in range(tn // LANES):
        acc_ref[c] = acc[:, c * LANES:(c + 1) * LANES]
    for r in range(dil):
        for c in range(tn // LANES):
            o_ref[r, :, c * LANES:(c + 1) * LANES] = acc_ref[c, pl.ds(r, tm // dil, stride=dil), :].astype(o_ref.dtype)


def _matmul_by_residue(x, w, nb, seq, dil, name):
    m, k = x.shape
    n = w.shape[1]
    tm, tn = _mm_tiles(m, n)
    assert seq % tm == 0 and tm % (dil * 16) == 0
    per = seq // tm
    kern = functools.partial(_mm_residue_kernel, dil=dil)
    return pl.pallas_call(
        kern,
        grid=(m // tm, n // tn),
        in_specs=[pl.BlockSpec((tm, k), lambda i, j: (i, 0)),
                  pl.BlockSpec((k, tn), lambda i, j: (0, j))],
        out_specs=pl.BlockSpec((None, dil, tm // dil, tn), lambda i, j: (i // per, 0, i % per, j)),
        out_shape=jax.ShapeDtypeStruct((nb, dil, seq // dil, n), BF16),
        scratch_shapes=[pltpu.VMEM((tn // LANES, tm, LANES), F32)],
        compiler_params=_params("parallel", "arbitrary"),
        name=name,
    )(x, w)


def _matmul_sigmoid(x, w, b, name):
    m, k = x.shape
    n = w.shape[1]
    tm, tn = _mm_tiles(m, n)
    return pl.pallas_call(
        _mm_sigmoid_kernel,
        grid=(m // tm, n // tn),
        in_specs=[pl.BlockSpec((tm, k), lambda i, j: (i, 0)),
                  pl.BlockSpec((k, tn), lambda i, j: (0, j)),
                  pl.BlockSpec((1, tn), lambda i, j: (0, j))],
        out_specs=pl.BlockSpec((tm, tn), lambda i, j: (i, j)),
        out_shape=jax.ShapeDtypeStruct((m, n), BF16),
        compiler_params=_params("parallel", "arbitrary"),
        name=name,
    )(x, w, b.reshape(1, n))


def _matmul_residual(x, w, resid, gate, seq, name):
    m, k = x.shape
    n = w.shape[1]
    tm, tn = _mm_tiles(m, n)
    assert seq % tm == 0
    per = seq // tm
    return pl.pallas_call(
        _mm_resid_kernel,
        grid=(m // tm, n // tn),
        in_specs=[pl.BlockSpec((tm, k), lambda i, j: (i, 0)),
                  pl.BlockSpec((k, tn), lambda i, j: (0, j)),
                  pl.BlockSpec((tm, tn), lambda i, j: (i, j)),
                  pl.BlockSpec((None, 1, tn), lambda i, j: (i // per, 0, j))],
        out_specs=pl.BlockSpec((tm, tn), lambda i, j: (i, j)),
        out_shape=jax.ShapeDtypeStruct((m, n), F32),
        compiler_params=_params("parallel", "arbitrary"),
        name=name,
    )(x, w, resid, gate)


def _attn_a_kernel(q_ref, k_ref, v_ref, gq_ref, gk_ref, lam_ref, sub_ref, slope_ref, o_ref, kn_ref,
                   *, lam_init, tq, seq):
    qi = pl.program_id(2)

    @pl.when(qi == 0)
    def _():
        kn_ref[...] = _head_norm(k_ref[...].astype(F32), gk_ref[...]).astype(BF16)

    lv = lam_ref[...]
    s01 = jnp.sum(lv[0:1] * lv[1:2], axis=-1, keepdims=True)
    s23 = jnp.sum(lv[2:3] * lv[3:4], axis=-1, keepdims=True)
    lam = jnp.exp(s01) - jnp.exp(s23) + lam_init

    qn = _head_norm(q_ref[...].astype(F32), gq_ref[...])
    q2 = _split_heads(qn)
    sc = _qk(q2, kn_ref[...])
    row = qi * tq + lax.broadcasted_iota(I32, (tq, seq), 0)
    col = lax.broadcasted_iota(I32, (tq, seq), 1)
    bias = slope_ref[...] * jnp.abs(row - col).astype(F32)
    v = v_ref[...]

    def component(s):
        s = s - bias
        m = jnp.max(s, axis=-1, keepdims=True)
        p = jnp.exp(s - m)
        l = jnp.sum(p, axis=-1, keepdims=True)
        return jnp.dot(p.astype(BF16), v, preferred_element_type=F32) / l

    o = component(sc[:tq]) - lam * component(sc[tq:])
    ms = jnp.mean(o * o, axis=-1, keepdims=True)
    o_ref[...] = (o * lax.rsqrt(ms + RMS_EPS) * sub_ref[...] * (1.0 - lam_init)).astype(o_ref.dtype)


def _attn_a(qkv, gq, gk, a_lambda, a_subln, lam_init):
    nb, seq, _ = qkv.shape
    tq = 256
    per = SEG_W // LANES
    slopes = jnp.asarray(_alibi_slopes(A_HEADS)).reshape(A_HEADS, 1, 1)
    kern = functools.partial(_attn_a_kernel, lam_init=lam_init, tq=tq, seq=seq)
    return pl.pallas_call(
        kern,
        grid=(nb, A_HEADS, seq // tq),
        in_specs=[pl.BlockSpec((None, tq, LANES), lambda b, h, i: (b, i, SEG_AQ * per + h)),
                  pl.BlockSpec((None, seq, LANES), lambda b, h, i: (b, 0, SEG_AK * per + h)),
                  pl.BlockSpec((None, seq, LANES), lambda b, h, i: (b, 0, SEG_AV * per + h)),
                  pl.BlockSpec((1, LANES), lambda b, h, i: (0, 0)),
                  pl.BlockSpec((1, LANES), lambda b, h, i: (0, 0)),
                  pl.BlockSpec((4, HEAD_DIM), lambda b, h, i: (0, 0)),
                  pl.BlockSpec((1, LANES), lambda b, h, i: (0, 0)),
                  pl.BlockSpec((None, 1, 1), lambda b, h, i: (h, 0, 0))],
        out_specs=pl.BlockSpec((None, tq, LANES), lambda b, h, i: (b, i, h)),
        out_shape=jax.ShapeDtypeStruct((nb, seq, BRANCH_W), BF16),
        scratch_shapes=[pltpu.VMEM((seq, LANES), BF16)],
        compiler_params=_params("parallel", "arbitrary", "arbitrary"),
        name="attn_a",
    )(qkv, qkv, qkv, gq, gk, a_lambda, a_subln.reshape(1, LANES), slopes)


def _attn_b_kernel(q_ref, k_ref, v_ref, gq_ref, gk_ref, slope_ref, sink_ref, o_ref, k2_ref, v2_ref,
                   *, tq, nk, seq):
    j = pl.program_id(1)
    qi = pl.program_id(2)

    @pl.when(qi == 0)
    def _():
        half = (lax.broadcasted_iota(I32, (seq, LANES), 1) >= HEAD_DIM).astype(I32)
        own = half == j
        kn = _head_norm(k_ref[...].astype(F32), gk_ref[...])
        k2_ref[...] = jnp.where(own, kn, pltpu.roll(kn, HEAD_DIM, axis=1)).astype(BF16)
        v = v_ref[...].astype(F32)
        v2_ref[...] = jnp.where(own, v, pltpu.roll(v, HEAD_DIM, axis=1)).astype(BF16)

    ws = pl.multiple_of(jnp.clip(qi * tq - B_WINDOW, 0, seq - nk), LANES)
    kw = k2_ref[pl.ds(ws, nk), :]
    vw = v2_ref[pl.ds(ws, nk), :]
    q = q_ref[...].astype(F32)
    g = gq_ref[...]
    qa = _head_norm(q[:, :LANES], g)
    qb = _head_norm(q[:, LANES:], g)
    q4 = jnp.concatenate([_split_heads(qa), _split_heads(qb)], axis=0)
    sc = _qk(q4, kw)
    row = qi * tq + lax.broadcasted_iota(I32, (tq, nk), 0)
    col = ws + lax.broadcasted_iota(I32, (tq, nk), 1)
    rel = jnp.abs(row - col)
    valid = rel <= B_WINDOW
    relf = rel.astype(F32)
    outs = []
    for gi in range(B_HEADS // B_KV_HEADS):
        s = jnp.where(valid, sc[gi * tq:(gi + 1) * tq] - slope_ref[gi] * relf, NEG_INF)
        m = jnp.max(s, axis=-1, keepdims=True)
        p = jnp.exp(s - m)
        l = jnp.sum(p, axis=-1, keepdims=True) + jnp.exp(sink_ref[gi] - m)
        outs.append(jnp.dot(p.astype(BF16), vw, preferred_element_type=F32) / l)
    lo = _lo_lanes((tq, LANES))
    o = jnp.concatenate([jnp.where(lo, outs[0], outs[1]), jnp.where(lo, outs[2], outs[3])], axis=1)
    o_ref[...] = o.astype(o_ref.dtype)


def _attn_b(qkv, kvb, gq, gk, b_sink):
    nb, seq, _ = qkv.shape
    tq = 128
    nk = tq + 2 * B_WINDOW
    grp = B_HEADS // B_KV_HEADS
    qw = grp * HEAD_DIM
    slopes = jnp.asarray(_alibi_slopes(B_HEADS)).reshape(B_HEADS, 1, 1)
    kern = functools.partial(_attn_b_kernel, tq=tq, nk=nk, seq=seq)
    return pl.pallas_call(
        kern,
        grid=(nb, B_KV_HEADS, seq // tq),
        in_specs=[pl.BlockSpec((None, tq, qw), lambda b, j, i: (b, i, SEG_BQ * (SEG_W // qw) + j)),
                  pl.BlockSpec((None, seq, LANES), lambda b, j, i: (b, 0, 0)),
                  pl.BlockSpec((None, seq, LANES), lambda b, j, i: (b, 0, 1)),
                  pl.BlockSpec((1, LANES), lambda b, j, i: (0, 0)),
                  pl.BlockSpec((1, LANES), lambda b, j, i: (0, 0)),
                  pl.BlockSpec((grp, 1, 1), lambda b, j, i: (j, 0, 0)),
                  pl.BlockSpec((grp, 1, 1), lambda b, j, i: (j, 0, 0))],
        out_specs=pl.BlockSpec((None, tq, qw), lambda b, j, i: (b, i, j)),
        out_shape=jax.ShapeDtypeStruct((nb, seq, BRANCH_W), BF16),
        scratch_shapes=[pltpu.VMEM((seq, LANES), BF16), pltpu.VMEM((seq, LANES), BF16)],
        compiler_params=_params("parallel", "arbitrary", "arbitrary"),
        name="attn_b",
    )(qkv, kvb, kvb, gq, gk, slopes, b_sink.reshape(B_HEADS, 1, 1))


def _attn_c_kernel(q_ref, k_ref, v_ref, gq_ref, gk_ref, slope_ref, o_ref, lse_ref, kn_ref,
                   *, tq, nk, sub, side, dil):
    qi = pl.program_id(2)
    n_pairs = SEG_W // LANES

    @pl.when(qi == 0)
    def _():
        for hp in range(n_pairs):
            cs = slice(hp * LANES, (hp + 1) * LANES)
            kn_ref[:, cs] = _head_norm(k_ref[:, cs].astype(F32), gk_ref[...]).astype(BF16)

    ws = pl.multiple_of(jnp.clip(qi * tq - side, 0, sub - nk), HEAD_DIM)
    row = qi * tq + lax.broadcasted_iota(I32, (tq, nk), 0)
    col = ws + lax.broadcasted_iota(I32, (tq, nk), 1)
    rel = jnp.abs(row - col)
    valid = rel <= side
    relf = (dil * rel).astype(F32)
    lo = _lo_lanes((tq, LANES))
    for hp in range(n_pairs):
        cs = slice(hp * LANES, (hp + 1) * LANES)
        qn = _head_norm(q_ref[:, cs].astype(F32), gq_ref[...])
        sc = _qk(_split_heads(qn), kn_ref[pl.ds(ws, nk), cs])
        vw = v_ref[pl.ds(ws, nk), cs]
        o2, l2 = [], []
        for t in range(2):
            s = jnp.where(valid, sc[t * tq:(t + 1) * tq] - slope_ref[2 * hp + t] * relf, NEG_INF)
            m = jnp.max(s, axis=-1, keepdims=True)
            p = jnp.exp(s - m)
            l = jnp.sum(p, axis=-1, keepdims=True)
            o2.append(jnp.dot(p.astype(BF16), vw, preferred_element_type=F32) / l)
            l2.append(m + jnp.log(l))
        o_ref[:, cs] = jnp.where(lo, o2[0], o2[1]).astype(o_ref.dtype)
        lse_ref[:, cs] = jnp.where(lo, l2[0], l2[1])


def _attn_c_group(src, seg, gi, gq, gk):
    nb, dil, sub, _ = src.shape
    win, gdil = C_DILATIONS[gi]
    assert gdil == dil
    side = win // (2 * dil)
    tq = min(256, sub)
    nk = min(tq + 2 * side, sub)
    slopes = jnp.asarray(_alibi_slopes(N_DIL * C_HEADS).reshape(N_DIL, C_HEADS)[gi]).reshape(C_HEADS, 1, 1)
    kern = functools.partial(_attn_c_kernel, tq=tq, nk=nk, sub=sub, side=side, dil=dil)
    return pl.pallas_call(
        kern,
        grid=(nb, dil, sub // tq),
        in_specs=[pl.BlockSpec((None, None, tq, SEG_W), lambda b, r, i: (b, r, i, seg)),
                  pl.BlockSpec((None, None, sub, SEG_W), lambda b, r, i: (b, r, 0, seg + 1)),
                  pl.BlockSpec((None, None, sub, SEG_W), lambda b, r, i: (b, r, 0, seg + 2)),
                  pl.BlockSpec((1, LANES), lambda b, r, i: (0, 0)),
                  pl.BlockSpec((1, LANES), lambda b, r, i: (0, 0)),
                  pl.BlockSpec((C_HEADS, 1, 1), lambda b, r, i: (0, 0, 0))],
        out_specs=[pl.BlockSpec((None, None, tq, SEG_W), lambda b, r, i: (b, r, i, 0)),
                   pl.BlockSpec((None, None, tq, SEG_W), lambda b, r, i: (b, r, i, 0))],
        out_shape=[jax.ShapeDtypeStruct((nb, dil, sub, SEG_W), BF16),
                   jax.ShapeDtypeStruct((nb, dil, sub, SEG_W), F32)],
        scratch_shapes=[pltpu.VMEM((sub, SEG_W), BF16)],
        compiler_params=_params("parallel", "arbitrary", "arbitrary"),
        name=f"attn_c{gi}",
    )(src, src, src, gq, gk, slopes)


def _attn_d_kernel(q_ref, k_ref, v_ref, gq_ref, gk_ref, bias_ref, o_ref, kn_ref, *, rows, nk, rb):
    blk = pl.program_id(1)
    n_pairs = SEG_W // LANES

    @pl.when(blk == 0)
    def _():
        for hp in range(n_pairs):
            cs = slice(hp * LANES, (hp + 1) * LANES)
            kn_ref[:, cs] = _head_norm(k_ref[:, cs].astype(F32), gk_ref[...]).astype(BF16)

    kr = nk // GRID_W
    lo = _lo_lanes((GRID_W, LANES))
    for rr in range(rb):
        r = blk * rb + rr
        kr0 = jnp.clip(r - kr // 2, 0, rows - kr)
        d0 = kr0 - r + NA_ROWS - 1
        ws = pl.multiple_of(kr0 * GRID_W, GRID_W)
        rs = slice(rr * GRID_W, (rr + 1) * GRID_W)
        for hp in range(n_pairs):
            cs = slice(hp * LANES, (hp + 1) * LANES)
            qn = _head_norm(q_ref[rs, cs].astype(F32), gq_ref[...])
            sc = _qk(_split_heads(qn), kn_ref[pl.ds(ws, nk), cs])
            vw = v_ref[pl.ds(ws, nk), cs]
            o2 = []
            for t in range(2):
                s = sc[t * GRID_W:(t + 1) * GRID_W] + bias_ref[d0, 2 * hp + t]
                m = jnp.max(s, axis=-1, keepdims=True)
                p = jnp.exp(s - m)
                l = jnp.sum(p, axis=-1, keepdims=True)
                o2.append(jnp.dot(p.astype(BF16), vw, preferred_element_type=F32) / l)
            o_ref[rs, cs] = jnp.where(lo, o2[0], o2[1]).astype(o_ref.dtype)


def _na_bias_table(rpb, rows):
    kr = min(NA_ROWS, rows)
    qc = np.arange(GRID_W)[:, None]
    kc = np.arange(GRID_W)[None, :]
    col_start = np.clip(qc - NA_COLS // 2, 0, GRID_W - NA_COLS)
    ok = (kc >= col_start) & (kc < col_start + NA_COLS)
    dc = np.clip(kc - qc + NA_COLS - 1, 0, 2 * NA_COLS - 2)
    drow = np.arange(kr)[:, None] + np.arange(kr)[None, :]
    t = rpb[:, drow][:, :, :, dc]
    t = jnp.where(ok[None, None, None], t.astype(F32), NEG_INF)
    t = t.transpose(1, 0, 3, 2, 4)
    return t.reshape(kr, D_HEADS, GRID_W, kr * GRID_W)


def _attn_d(qkv, gq, gk, d_rpb):
    nb, seq, _ = qkv.shape
    rows = seq // GRID_W
    kr = min(NA_ROWS, rows)
    nk = kr * GRID_W
    table = _na_bias_table(d_rpb, rows)
    rb = 4
    assert rows % rb == 0
    kern = functools.partial(_attn_d_kernel, rows=rows, nk=nk, rb=rb)
    return pl.pallas_call(
        kern,
        grid=(nb, rows // rb),
        in_specs=[pl.BlockSpec((None, rb * GRID_W, SEG_W), lambda b, r: (b, r, SEG_DQ)),
                  pl.BlockSpec((None, seq, SEG_W), lambda b, r: (b, 0, SEG_DK)),
                  pl.BlockSpec((None, seq, SEG_W), lambda b, r: (b, 0, SEG_DV)),
                  pl.BlockSpec((1, LANES), lambda b, r: (0, 0)),
                  pl.BlockSpec((1, LANES), lambda b, r: (0, 0)),
                  pl.BlockSpec((kr, D_HEADS, GRID_W, nk), lambda b, r: (0, 0, 0, 0))],
        out_specs=pl.BlockSpec((None, rb * GRID_W, SEG_W), lambda b, r: (b, r, 0)),
        out_shape=jax.ShapeDtypeStruct((nb, seq, BRANCH_W), BF16),
        scratch_shapes=[pltpu.VMEM((seq, SEG_W), BF16)],
        compiler_params=_params("parallel", "arbitrary"),
        name="attn_d",
    )(qkv, qkv, qkv, gq, gk, table)


def _to_token_order(blk_ref, scr_ref):
    dil, n, _ = blk_ref.shape
    for r in range(dil):
        for c in range(SEG_W // LANES):
            scr_ref[c, pl.ds(r, n, stride=dil), :] = blk_ref[r, :, c * LANES:(c + 1) * LANES].astype(F32)


def _merge_kernel(oa_ref, ob_ref, oc0_ref, oc1_ref, oc2_ref, l0_ref, l1_ref, l2_ref, od_ref,
                  gates_ref, wb_ref, o_ref, s_o1, s_l1, s_o2, s_l2):
    _to_token_order(oc1_ref, s_o1)
    _to_token_order(l1_ref, s_l1)
    _to_token_order(oc2_ref, s_o2)
    _to_token_order(l2_ref, s_l2)
    slabs = []
    for c in range(SEG_W // LANES):
        cs = slice(c * LANES, (c + 1) * LANES)
        l0, l1, l2 = l0_ref[:, cs], s_l1[c], s_l2[c]
        lm = jnp.maximum(jnp.maximum(l0, l1), l2)
        e0, e1, e2 = jnp.exp(l0 - lm), jnp.exp(l1 - lm), jnp.exp(l2 - lm)
        oc = (e0 * oc0_ref[:, cs].astype(F32) + e1 * s_o1[c] + e2 * s_o2[c]) / (e0 + e1 + e2)
        slabs.append(oc.astype(BF16))
    branches = (oa_ref[...], ob_ref[...], jnp.concatenate(slabs, axis=1), od_ref[...])
    acc = None
    for n, br in enumerate(branches):
        proj = jnp.dot(br, wb_ref[n], preferred_element_type=F32)
        term = gates_ref[:, n * D_MODEL:(n + 1) * D_MODEL].astype(F32) * proj
        acc = term if acc is None else acc + term
    o_ref[...] = acc.astype(o_ref.dtype)


def _merge(oa, ob, oc, lses, od, gates, w_branch):
    nb, seq, _ = oa.shape
    tm = 256
    per = seq // tm
    tok = pl.BlockSpec((None, tm, BRANCH_W), lambda b, i: (b, i, 0))

    def res(dil):
        return pl.BlockSpec((None, dil, tm // dil, BRANCH_W), lambda b, i: (b, 0, i, 0))

    dils = [d for _, d in C_DILATIONS]
    assert dils[0] == 1
    res0 = pl.BlockSpec((None, None, tm, BRANCH_W), lambda b, i: (b, 0, i, 0))
    slab = pltpu.VMEM((BRANCH_W // LANES, tm, LANES), F32)
    return pl.pallas_call(
        _merge_kernel,
        grid=(nb, per),
        in_specs=[tok, tok, res0, res(dils[1]), res(dils[2]), res0, res(dils[1]), res(dils[2]), tok,
                  pl.BlockSpec((tm, N_BRANCH * D_MODEL), lambda b, i: (b * per + i, 0)),
                  pl.BlockSpec((N_BRANCH, BRANCH_W, D_MODEL), lambda b, i: (0, 0, 0))],
        out_specs=pl.BlockSpec((tm, D_MODEL), lambda b, i: (b * per + i, 0)),
        out_shape=jax.ShapeDtypeStruct((nb * seq, D_MODEL), BF16),
        scratch_shapes=[slab, slab, slab, slab],
        compiler_params=_params("parallel", "arbitrary"),
        name="merge",
    )(oa, ob, oc[0], oc[1], oc[2], lses[0], lses[1], lses[2], od, gates, w_branch)


def _route_kernel(x_ref, g_ref, sc_ref, sh_ref, wr_ref, br_ref, hf_ref, idx_ref, p_ref):
    h = _modulated_norm(x_ref[...], g_ref, sc_ref, sh_ref)
    _store_row_tiles(hf_ref, h)
    logits = jnp.dot(h.astype(BF16), wr_ref[...], preferred_element_type=F32) + br_ref[...]
    lane = lax.broadcasted_iota(I32, logits.shape, 1)
    idx_t = jnp.zeros(logits.shape, I32)
    e_t = jnp.zeros(logits.shape, F32)
    den = None
    top = None
    for k in range(TOP_K):
        m = jnp.max(logits, axis=-1, keepdims=True)
        idx = jnp.min(jnp.where(logits == m, lane, LANES), axis=-1, keepdims=True)
        logits = jnp.where(lane == idx, -jnp.inf, logits)
        if k == 0:
            top = m
        e = jnp.exp(m - top)
        den = e if den is None else den + e
        idx_t = jnp.where(lane == k, idx, idx_t)
        e_t = jnp.where(lane == k, e, e_t)
    idx_ref[...] = idx_t
    p_ref[...] = e_t / den


def _route(x, g, sc, sh, w_router, b_router):
    nb, s, d = x.shape
    tm = 256
    wr = jnp.zeros((d, LANES), BF16).at[:, :N_EXPERTS].set(w_router.astype(BF16))
    br = jnp.full((1, LANES), NEG_INF, F32).at[0, :N_EXPERTS].set(b_router)
    per = s // tm
    return pl.pallas_call(
        _route_kernel,
        grid=(nb, per),
        in_specs=[pl.BlockSpec((None, tm, d), lambda b, i: (b, i, 0)),
                  pl.BlockSpec((1, d), lambda b, i: (0, 0)),
                  pl.BlockSpec((None, 1, d), lambda b, i: (b, 0, 0)),
                  pl.BlockSpec((None, 1, d), lambda b, i: (b, 0, 0)),
                  pl.BlockSpec((d, LANES), lambda b, i: (0, 0)),
                  pl.BlockSpec((1, LANES), lambda b, i: (0, 0))],
        out_specs=[pl.BlockSpec((tm * ROW_TILE, LANES), lambda b, i: (b * per + i, 0)),
                   pl.BlockSpec((tm, LANES), lambda b, i: (b * per + i, 0)),
                   pl.BlockSpec((tm, LANES), lambda b, i: (b * per + i, 0))],
        out_shape=[jax.ShapeDtypeStruct((nb * s * ROW_TILE, LANES), U32),
                   jax.ShapeDtypeStruct((nb * s, LANES), I32),
                   jax.ShapeDtypeStruct((nb * s, LANES), F32)],
        compiler_params=_params("parallel", "parallel"),
        name="route",
    )(x, g.reshape(1, d), sc, sh, wr, br)


def _tile_rows(t):
    return pl.ds(pl.multiple_of(t * ROW_TILE, ROW_TILE), ROW_TILE)


def _dispatch_kernel(dest_ref, src_ref, init_ref, o_ref, sem, *, rows):
    del init_ref

    def row_copy(t, d):
        return pltpu.make_async_copy(src_ref.at[_tile_rows(t), :], o_ref.at[_tile_rows(d), :], sem)

    def issue(t, c):
        for k in range(TOP_K):
            row_copy(t, dest_ref[0, 0, t * TOP_K + k]).start(priority=k % DMA_QUEUES)
        return c
    lax.fori_loop(0, rows, issue, 0, unroll=DMA_UNROLL // TOP_K)

    def drain(t, c):
        for k in range(TOP_K):
            row_copy(t, 0).wait()
        return c
    lax.fori_loop(0, rows, drain, 0, unroll=DMA_UNROLL // TOP_K)


def _dispatch_rows(src, dest, n_slots):
    rows = DISPATCH_ROWS
    n_tok = src.shape[0] // ROW_TILE
    nblk = n_tok // rows
    kern = functools.partial(_dispatch_kernel, rows=rows)
    return pl.pallas_call(
        kern,
        grid=(nblk,),
        in_specs=[pl.BlockSpec((1, 1, rows * TOP_K), lambda i: (i, 0, 0), memory_space=pltpu.SMEM),
                  pl.BlockSpec((rows * ROW_TILE, LANES), lambda i: (i, 0)),
                  pl.BlockSpec(memory_space=pl.ANY)],
        out_specs=pl.BlockSpec(memory_space=pl.ANY),
        out_shape=jax.ShapeDtypeStruct((n_slots * ROW_TILE, LANES), U32),
        scratch_shapes=[pltpu.SemaphoreType.DMA(())],
        input_output_aliases={2: 0},
        compiler_params=_params("arbitrary"),
        name="moe_dispatch",
    )(dest.reshape(nblk, 1, rows * TOP_K), src, jnp.zeros((n_slots * ROW_TILE, LANES), U32))


def _expert_kernel(be_ref, nu_ref, xs_ref, wg_ref, wu_ref, wdl_ref, wdh_ref, bg_ref, bu_ref, bdl_ref, bdh_ref,
                   o_ref, xb_ref, a_ref, *, n_up, n_down):
    i = pl.program_id(0)
    f = pl.program_id(1)
    used = i < nu_ref[0]
    rows = xb_ref.shape[0]
    tf = wg_ref.shape[1]

    @pl.when(used & (f == 0))
    def _():
        for c in range(ROW_TILE):
            lo, hi = _load_row_tile_chunk(xs_ref, c)
            xb_ref[:, c * LANES:(c + 1) * LANES] = lo.astype(BF16)
            xb_ref[:, HALF_D + c * LANES:HALF_D + (c + 1) * LANES] = hi.astype(BF16)

    halves = [slice(h * (rows // 2), (h + 1) * (rows // 2)) for h in range(2)]

    @pl.when(used & (f < n_up))
    def _():
        wg = wg_ref[...].astype(BF16)
        wu = wu_ref[...].astype(BF16)
        acts = []
        for rs in halves:
            x = xb_ref[rs, :]
            g = jnp.dot(x, wg, preferred_element_type=F32) + bg_ref[...]
            u = jnp.dot(x, wu, preferred_element_type=F32) + bu_ref[...]
            g = jnp.minimum(g, SWIGLU_LIMIT)
            u = jnp.clip(u, -SWIGLU_LIMIT, SWIGLU_LIMIT)
            acts.append(((u + 1.0) * g * jax.nn.sigmoid(SWIGLU_ALPHA * g)).astype(BF16))
        for ff in range(n_up):
            @pl.when(f == ff)
            def _(ff=ff):
                for rs, a in zip(halves, acts):
                    a_ref[rs, ff * tf:(ff + 1) * tf] = a

    @pl.when(used & (f >= n_up))
    def _():
        m = f - n_up
        wdl = wdl_ref[...].astype(BF16)
        wdh = wdh_ref[...].astype(BF16)
        per = wdl.shape[1] // LANES
        for h, rs in enumerate(halves):
            a = a_ref[rs, :]
            lo = jnp.dot(a, wdl, preferred_element_type=F32) + bdl_ref[...]
            hi = jnp.dot(a, wdh, preferred_element_type=F32) + bdh_ref[...]
            packed = _pack_pair(lo, hi)
            for j in range(per):
                start = h * (rows // 2) * ROW_TILE + m * per + j
                o_ref[pl.ds(start, rows // 2, stride=ROW_TILE), :] = packed[:, j * LANES:(j + 1) * LANES]

    @pl.when(jnp.logical_not(used) & (f == n_up + n_down - 1))
    def _():
        o_ref[...] = jnp.zeros(o_ref.shape, o_ref.dtype)


def _experts(xs, block_expert, n_used, layer, w_gate_up, b_gate_up, w_down, b_down):
    rows, tf, tn = MOE_ROWS, MOE_TF, MOE_TN
    n_slots = xs.shape[0] // ROW_TILE
    nblk = n_slots // rows
    n_up = D_FF // tf
    n_down = HALF_D // tn
    depth, e = w_gate_up.shape[:2]
    kern = functools.partial(_expert_kernel, n_up=n_up, n_down=n_down)

    def up(i, f, nu):
        return jnp.where(i < nu[0], jnp.minimum(f, n_up - 1), n_up - 1)

    def down(i, f, nu):
        return jnp.where((i < nu[0]) & (f >= n_up // 2), jnp.maximum(f - n_up, 0), n_down - 1)

    return pl.pallas_call(
        kern,
        grid_spec=pltpu.PrefetchScalarGridSpec(
            num_scalar_prefetch=2,
            grid=(nblk, n_up + n_down),
            in_specs=[pl.BlockSpec((rows * ROW_TILE, LANES), lambda i, f, be, nu: (i, 0)),
                      pl.BlockSpec((None, None, D_MODEL, tf), lambda i, f, be, nu: (layer, be[i], 0, up(i, f, nu))),
                      pl.BlockSpec((None, None, D_MODEL, tf),
                                   lambda i, f, be, nu: (layer, be[i], 0, n_up + up(i, f, nu))),
                      pl.BlockSpec((None, None, D_FF, tn), lambda i, f, be, nu: (layer, be[i], 0, down(i, f, nu))),
                      pl.BlockSpec((None, None, D_FF, tn),
                                   lambda i, f, be, nu: (layer, be[i], 0, n_down + down(i, f, nu))),
                      pl.BlockSpec((None, None, 1, tf), lambda i, f, be, nu: (layer, be[i], 0, up(i, f, nu))),
                      pl.BlockSpec((None, None, 1, tf),
                                   lambda i, f, be, nu: (layer, be[i], 0, n_up + up(i, f, nu))),
                      pl.BlockSpec((None, None, 1, tn), lambda i, f, be, nu: (layer, be[i], 0, down(i, f, nu))),
                      pl.BlockSpec((None, None, 1, tn),
                                   lambda i, f, be, nu: (layer, be[i], 0, n_down + down(i, f, nu)))],
            out_specs=pl.BlockSpec((rows * ROW_TILE, LANES), lambda i, f, be, nu: (i, 0)),
            scratch_shapes=[pltpu.VMEM((rows, D_MODEL), BF16), pltpu.VMEM((rows, D_FF), BF16)],
        ),
        out_shape=jax.ShapeDtypeStruct((n_slots * ROW_TILE, LANES), U32),
        compiler_params=_params("arbitrary", "arbitrary"),
        name="moe_experts",
    )(block_expert, n_used, xs, w_gate_up, w_gate_up, w_down, w_down,
      b_gate_up.reshape(depth, e, 1, 2 * D_FF), b_gate_up.reshape(depth, e, 1, 2 * D_FF),
      b_down.reshape(depth, e, 1, D_MODEL), b_down.reshape(depth, e, 1, D_MODEL))


def _combine_kernel(dest_ref, src_ref, p_ref, x_ref, g_ref, o_ref, buf_ref, sem, *, rows):
    def row_copy(k, t, d):
        return pltpu.make_async_copy(src_ref.at[_tile_rows(d), :], buf_ref.at[k, _tile_rows(t), :], sem)

    for k in range(TOP_K):
        def issue(t2, c, k=k):
            for par in range(DMA_QUEUES):
                t = t2 * DMA_QUEUES + par
                row_copy(k, t, dest_ref[0, 0, t * TOP_K + k]).start(priority=par)
            return c
        lax.fori_loop(0, rows // DMA_QUEUES, issue, 0, unroll=DMA_UNROLL // DMA_QUEUES)
    for k in range(TOP_K):
        def drain(t, c, k=k):
            row_copy(k, t, 0).wait()
            return c
        lax.fori_loop(0, rows, drain, 0, unroll=DMA_UNROLL)

    p = p_ref[...]
    pk = [p[:, k:k + 1] for k in range(TOP_K)]
    for c in range(ROW_TILE):
        y_lo = None
        y_hi = None
        for k in range(TOP_K):
            lo, hi = _load_row_tile_chunk(buf_ref.at[k], c)
            y_lo = pk[k] * lo if y_lo is None else y_lo + pk[k] * lo
            y_hi = pk[k] * hi if y_hi is None else y_hi + pk[k] * hi
        cl = slice(c * LANES, (c + 1) * LANES)
        ch = slice(HALF_D + c * LANES, HALF_D + (c + 1) * LANES)
        o_ref[:, cl] = x_ref[:, cl] + g_ref[:, cl] * y_lo
        o_ref[:, ch] = x_ref[:, ch] + g_ref[:, ch] * y_hi


def _combine(outs, dest, probs, x, gate, seq):
    t, d = x.shape
    rows = COMBINE_ROWS
    nblk = t // rows
    per = seq // rows
    kern = functools.partial(_combine_kernel, rows=rows)
    return pl.pallas_call(
        kern,
        grid=(nblk,),
        in_specs=[pl.BlockSpec((1, 1, rows * TOP_K), lambda i: (i, 0, 0), memory_space=pltpu.SMEM),
                  pl.BlockSpec(memory_space=pl.ANY),
                  pl.BlockSpec((rows, LANES), lambda i: (i, 0)),
                  pl.BlockSpec((rows, d), lambda i: (i, 0)),
                  pl.BlockSpec((None, 1, d), lambda i: (i // per, 0, 0))],
        out_specs=pl.BlockSpec((rows, d), lambda i: (i, 0)),
        out_shape=jax.ShapeDtypeStruct((t, d), F32),
        scratch_shapes=[pltpu.VMEM((TOP_K, rows * ROW_TILE, LANES), U32), pltpu.SemaphoreType.DMA(())],
        compiler_params=_params("arbitrary"),
        name="moe_combine",
    )(dest.reshape(nblk, 1, rows * TOP_K), outs, probs, x, gate)


def _routing_plan(top_idx, n_tok):
    expert = top_idx.reshape(-1)
    onehot = jax.nn.one_hot(expert, N_EXPERTS, dtype=I32)
    csum = jnp.cumsum(onehot, axis=0)
    rank = jnp.take_along_axis(csum, expert[:, None], axis=1)[:, 0] - 1
    padded = (csum[-1] + MOE_ROWS - 1) // MOE_ROWS * MOE_ROWS
    pad_end = jnp.cumsum(padded)
    dest = (pad_end[expert] - padded[expert] + rank).astype(I32)
    n_blocks = -(-(n_tok * TOP_K) // MOE_ROWS) + N_EXPERTS
    block_expert = jnp.minimum(
        jnp.searchsorted(pad_end, jnp.arange(n_blocks, dtype=I32) * MOE_ROWS, side='right'),
        N_EXPERTS - 1).astype(I32)
    n_blocks_used = (pad_end[-1:] // MOE_ROWS).astype(I32)
    return dest, block_expert, n_blocks_used, n_blocks * MOE_ROWS


def _moe(x, g, sc, sh, gate, layer, w_router, b_router, w_gate_up, b_gate_up, w_down, b_down):
    nb, seq, d = x.shape
    n_tok = nb * seq
    hf, idx_t, p_t = _route(x, g, sc, sh, w_router, b_router)
    dest, block_expert, n_blocks_used, n_slots = _routing_plan(idx_t[:, :TOP_K], n_tok)
    xs = _dispatch_rows(hf, dest, n_slots)
    outs = _experts(xs, block_expert, n_blocks_used, layer, w_gate_up, b_gate_up, w_down, b_down)
    y = _combine(outs, dest, p_t, x.reshape(n_tok, d), gate, seq)
    return y.reshape(nb, seq, d)


def _pair_gain(g, scale=1.0):
    return (jnp.tile(g.astype(F32), LANES // HEAD_DIM) * scale).reshape(1, LANES)


def _token_mixer(x, layer, lam_init, g, sc, sh, gate, p):
    nb, seq, d = x.shape
    n_tok = nb * seq
    hm = _normmod(x, g, sc, sh).reshape(n_tok, d)
    w_in = p['w_in'][layer]
    w_main = jnp.concatenate([w_in[:, :BKV_LO], w_in[:, C_LO:C_LO + C_GROUP_W], w_in[:, D_LO:]],
                             axis=1).astype(BF16)
    qkv = _matmul(hm, w_main, "in_proj").reshape(nb, seq, QKV_W)
    kvb = _matmul(hm, w_in[:, BKV_LO:BKV_HI].astype(BF16), "in_proj_bkv").reshape(nb, seq, BKV_HI - BKV_LO)
    c_src = [(qkv.reshape(nb, 1, seq, QKV_W), SEG_C0)]
    for gi in range(1, N_DIL):
        lo = C_LO + gi * C_GROUP_W
        c_src.append((_matmul_by_residue(hm, w_in[:, lo:lo + C_GROUP_W].astype(BF16), nb, seq,
                                         C_DILATIONS[gi][1], f"in_proj_c{gi}"), 0))
    qs = HEAD_DIM ** -0.5
    oa = _attn_a(qkv, _pair_gain(p['a_qk_norm'][layer, 0], qs), _pair_gain(p['a_qk_norm'][layer, 1]),
                 p['a_lambda'][layer], p['a_subln'][layer], lam_init)
    ob = _attn_b(qkv, kvb, _pair_gain(p['b_qk_norm'][layer, 0], qs), _pair_gain(p['b_qk_norm'][layer, 1]),
                 p['b_sink'][layer])
    ocs, lses = [], []
    for gi in range(N_DIL):
        o, lse = _attn_c_group(c_src[gi][0], c_src[gi][1], gi, _pair_gain(p['c_qk_norm'][layer, gi, 0], qs),
                               _pair_gain(p['c_qk_norm'][layer, gi, 1]))
        ocs.append(o)
        lses.append(lse)
    od = _attn_d(qkv, _pair_gain(p['d_qk_norm'][layer, 0], qs), _pair_gain(p['d_qk_norm'][layer, 1]),
                 p['d_rpb'][layer])
    gates = _matmul_sigmoid(hm, p['w_gate'][layer].astype(BF16), p['b_gate'][layer], "gate_proj")
    mixed = _merge(oa, ob, ocs, lses, od, gates, p['w_branch'][layer].astype(BF16))
    y = _matmul_residual(mixed, p['w_out'][layer].astype(BF16), x.reshape(n_tok, d), gate, seq, "out_proj")
    return y.reshape(nb, seq, d)


def kernel(x_prompt, x_sample, c_prompt, c_sample, w_ada, b_ada, norm_g, w_in, a_qk_norm, a_lambda, a_subln, b_qk_norm, b_sink, c_qk_norm, d_qk_norm, d_rpb, w_gate, b_gate, w_branch, w_out, w_router, b_router, w_gate_up, b_gate_up, w_down, b_down):
    p = dict(w_in=w_in, a_qk_norm=a_qk_norm, a_lambda=a_lambda, a_subln=a_subln, b_qk_norm=b_qk_norm,
             b_sink=b_sink, c_qk_norm=c_qk_norm, d_qk_norm=d_qk_norm, d_rpb=d_rpb, w_gate=w_gate,
             b_gate=b_gate, w_branch=w_branch, w_out=w_out)
    assert x_prompt.shape[1:] == x_sample.shape[1:]
    n_prompt = x_prompt.shape[0]
    x = jnp.concatenate([x_prompt, x_sample], axis=0)
    c = jnp.concatenate([c_prompt, c_sample], axis=0)
    nb, seq, d = x.shape
    depth = w_ada.shape[0]
    mod = _ada_mod(c, w_ada, b_ada).reshape(depth, nb, 6, 1, d)
    for layer in range(depth):
        sh1, sc1, g1, sh2, sc2, g2 = (mod[layer, :, i] for i in range(6))
        lam_init = 0.8 - 0.6 * math.exp(-0.3 * layer)
        x = _token_mixer(x, layer, lam_init, norm_g[layer, 0], sc1, sh1, g1, p)
        x = _moe(x, norm_g[layer, 1], sc2, sh2, g2, layer, w_router[layer], b_router[layer],
                 w_gate_up, b_gate_up, w_down, b_down)
    return x[:n_prompt], x[n_prompt:]
```

```python
import functools
import math

import numpy as np
import jax
import jax.numpy as jnp
from jax import lax
from jax.experimental import pallas as pl
from jax.experimental.pallas import tpu as pltpu

F32 = jnp.float32
BF16 = jnp.bfloat16
U32 = jnp.uint32
I32 = jnp.int32

D_MODEL = 2048
DEPTH = 4
HEAD_DIM = 64
N_BRANCH = 4
BRANCH_W = D_MODEL // N_BRANCH
A_HEADS = 4
B_HEADS = 8
B_KV_HEADS = 2
B_WINDOW = 128
C_HEADS = 8
C_DILATIONS = ((128, 1), (512, 4), (2048, 16))
N_DIL = 3
GRID_W = 64
D_HEADS = 8
NA_ROWS = 8
NA_COLS = 16
N_EXPERTS = 32
TOP_K = 4
D_FF = D_MODEL
SWIGLU_ALPHA = 1.702
SWIGLU_LIMIT = 7.0
RMS_EPS = 1e-6
NEG_INF = -1e30

LANES = 128
HALF_D = D_MODEL // 2
SEG_W = 512
SEG_AQ, SEG_AK, SEG_AV, SEG_BQ = 0, 1, 2, 3
SEG_C0 = 4
SEG_DQ, SEG_DK, SEG_DV = 7, 8, 9
N_SEG = 10
QKV_W = N_SEG * SEG_W
BKV_LO, BKV_HI = 2048, 2304
C_LO = 2304
C_GROUP_W = 3 * SEG_W
D_LO = C_LO + N_DIL * C_GROUP_W
ROW_TILE = 8

VMEM_LIMIT = 56 * 1024 * 1024

MOE_ROWS = 1024
MOE_TF = 512
MOE_TN = 256
DISPATCH_ROWS = 256
COMBINE_ROWS = 256
DMA_UNROLL = 8
DMA_QUEUES = 2


def _params(*sem):
    return pltpu.CompilerParams(dimension_semantics=sem, vmem_limit_bytes=VMEM_LIMIT)


def _alibi_slopes(n):
    return (2.0 ** (-8.0 * np.arange(1, n + 1, dtype=np.float64) / n)).astype(np.float32)


def _group_mean_sq(x):
    r = lax.broadcasted_iota(I32, (LANES, LANES), 0) // HEAD_DIM
    c = lax.broadcasted_iota(I32, (LANES, LANES), 1) // HEAD_DIM
    ones = jnp.where(r == c, 1.0, 0.0).astype(BF16)
    return jnp.dot((x * x).astype(BF16), ones, preferred_element_type=F32) * (1.0 / HEAD_DIM)


def _head_norm(x, gain):
    return x * lax.rsqrt(_group_mean_sq(x) + RMS_EPS) * gain


def _lo_lanes(shape):
    return lax.broadcasted_iota(I32, shape, 1) < HEAD_DIM


def _split_heads(x):
    lo = _lo_lanes(x.shape)
    return jnp.concatenate([jnp.where(lo, x, 0.0), jnp.where(lo, 0.0, x)], axis=0).astype(BF16)


def _qk(q, k):
    return lax.dot_general(q, k, (((1,), (1,)), ((), ())), preferred_element_type=F32)


def _pack_pair(a, b):
    ua = lax.bitcast_convert_type(a.astype(BF16).astype(F32), U32)
    ub = lax.bitcast_convert_type(b.astype(BF16).astype(F32), U32)
    return (ua >> 16) | (ub & jnp.uint32(0xFFFF0000))


def _unpack_pair(w):
    a = lax.bitcast_convert_type(w << 16, F32)
    b = lax.bitcast_convert_type(w & jnp.uint32(0xFFFF0000), F32)
    return a, b


def _store_row_tiles(ref, val):
    n = val.shape[0]
    for c in range(ROW_TILE):
        lo = val[:, c * LANES:(c + 1) * LANES]
        hi = val[:, HALF_D + c * LANES:HALF_D + (c + 1) * LANES]
        ref[pl.ds(c, n, stride=ROW_TILE), :] = _pack_pair(lo, hi)


def _load_row_tile_chunk(ref, c):
    n = ref.shape[0] // ROW_TILE
    return _unpack_pair(ref[pl.ds(c, n, stride=ROW_TILE), :])


def _ada_kernel(c_ref, w_ref, b_ref, o_ref):
    c = c_ref[...]
    h = (c * jax.nn.sigmoid(c)).astype(BF16)
    o_ref[...] = jnp.dot(h, w_ref[...].astype(BF16), preferred_element_type=F32) + b_ref[...]


def _ada_mod(c, w_ada, b_ada):
    nb, d = c.shape
    depth, _, n = w_ada.shape
    tn = 1024
    return pl.pallas_call(
        _ada_kernel,
        grid=(depth, n // tn),
        in_specs=[pl.BlockSpec((nb, d), lambda l, j: (0, 0)),
                  pl.BlockSpec((None, d, tn), lambda l, j: (l, 0, j)),
                  pl.BlockSpec((None, 1, tn), lambda l, j: (l, 0, j))],
        out_specs=pl.BlockSpec((None, nb, tn), lambda l, j: (l, 0, j)),
        out_shape=jax.ShapeDtypeStruct((depth, nb, n), F32),
        compiler_params=_params("parallel", "parallel"),
        name="ada_mod",
    )(c, w_ada, b_ada.reshape(depth, 1, n))


def _modulated_norm(x, g_ref, sc_ref, sh_ref):
    ms = jnp.mean(x * x, axis=-1, keepdims=True)
    y = x * lax.rsqrt(ms + RMS_EPS) * g_ref[...]
    return y * (1.0 + sc_ref[...]) + sh_ref[...]


def _normmod_kernel(x_ref, g_ref, sc_ref, sh_ref, o_ref):
    o_ref[...] = _modulated_norm(x_ref[...], g_ref, sc_ref, sh_ref).astype(o_ref.dtype)


def _normmod(x, g, sc, sh):
    nb, s, d = x.shape
    tm = 512
    return pl.pallas_call(
        _normmod_kernel,
        grid=(nb, s // tm),
        in_specs=[pl.BlockSpec((None, tm, d), lambda b, i: (b, i, 0)),
                  pl.BlockSpec((1, d), lambda b, i: (0, 0)),
                  pl.BlockSpec((None, 1, d), lambda b, i: (b, 0, 0)),
                  pl.BlockSpec((None, 1, d), lambda b, i: (b, 0, 0))],
        out_specs=pl.BlockSpec((None, tm, d), lambda b, i: (b, i, 0)),
        out_shape=jax.ShapeDtypeStruct((nb, s, d), BF16),
        compiler_params=_params("parallel", "parallel"),
        name="normmod",
    )(x, g.reshape(1, d), sc, sh)


def _mm_kernel(x_ref, w_ref, o_ref):
    o_ref[...] = jnp.dot(x_ref[...], w_ref[...], preferred_element_type=F32).astype(o_ref.dtype)


def _mm_sigmoid_kernel(x_ref, w_ref, b_ref, o_ref):
    acc = jnp.dot(x_ref[...], w_ref[...], preferred_element_type=F32) + b_ref[...]
    o_ref[...] = jax.nn.sigmoid(acc).astype(o_ref.dtype)


def _mm_resid_kernel(x_ref, w_ref, r_ref, g_ref, o_ref):
    acc = jnp.dot(x_ref[...], w_ref[...], preferred_element_type=F32)
    o_ref[...] = r_ref[...] + g_ref[...] * acc


def _mm_tiles(m, n):
    tm = min(1024, m)
    tn = 1024 if n % 1024 == 0 else min(512, n)
    assert m % tm == 0 and n % tn == 0
    return tm, tn


def _matmul(x, w, name):
    m, k = x.shape
    n = w.shape[1]
    tm, tn = _mm_tiles(m, n)
    return pl.pallas_call(
        _mm_kernel,
        grid=(m // tm, n // tn),
        in_specs=[pl.BlockSpec((tm, k), lambda i, j: (i, 0)),
                  pl.BlockSpec((k, tn), lambda i, j: (0, j))],
        out_specs=pl.BlockSpec((tm, tn), lambda i, j: (i, j)),
        out_shape=jax.ShapeDtypeStruct((m, n), BF16),
        compiler_params=_params("parallel", "arbitrary"),
        name=name,
    )(x, w)


def _mm_residue_kernel(x_ref, w_ref, o_ref, acc_ref, *, dil):
    tm, tn = acc_ref.shape[1], acc_ref.shape[0] * LANES
    acc = jnp.dot(x_ref[...], w_ref[...], preferred_element_type=F32)
    for c in range(tn // LANES):
        acc_ref[c] = acc[:, c * LANES:(c + 1) * LANES]
    for r in range(dil):
        for c in range(tn // LANES):
            o_ref[r, :, c * LANES:(c + 1) * LANES] = acc_ref[c, pl.ds(r, tm // dil, stride=dil), :].astype(o_ref.dtype)


def _matmul_by_residue(x, w, nb, seq, dil, name):
    m, k = x.shape
    n = w.shape[1]
    tm, tn = _mm_tiles(m, n)
    assert seq % tm == 0 and tm % (dil * 16) == 0
    per = seq // tm
    kern = functools.partial(_mm_residue_kernel, dil=dil)
    return pl.pallas_call(
        kern,
        grid=(m // tm, n // tn),
        in_specs=[pl.BlockSpec((tm, k), lambda i, j: (i, 0)),
                  pl.BlockSpec((k, tn), lambda i, j: (0, j))],
        out_specs=pl.BlockSpec((None, dil, tm // dil, tn), lambda i, j: (i // per, 0, i % per, j)),
        out_shape=jax.ShapeDtypeStruct((nb, dil, seq // dil, n), BF16),
        scratch_shapes=[pltpu.VMEM((tn // LANES, tm, LANES), F32)],
        compiler_params=_params("parallel", "arbitrary"),
        name=name,
    )(x, w)


def _matmul_sigmoid(x, w, b, name):
    m, k = x.shape
    n = w.shape[1]
    tm, tn = _mm_tiles(m, n)
    return pl.pallas_call(
        _mm_sigmoid_kernel,
        grid=(m // tm, n // tn),
        in_specs=[pl.BlockSpec((tm, k), lambda i, j: (i, 0)),
                  pl.BlockSpec((k, tn), lambda i, j: (0, j)),
                  pl.BlockSpec((1, tn), lambda i, j: (0, j))],
        out_specs=pl.BlockSpec((tm, tn), lambda i, j: (i, j)),
        out_shape=jax.ShapeDtypeStruct((m, n), BF16),
        compiler_params=_params("parallel", "arbitrary"),
        name=name,
    )(x, w, b.reshape(1, n))


def _matmul_residual(x, w, resid, gate, seq, name):
    m, k = x.shape
    n = w.shape[1]
    tm, tn = _mm_tiles(m, n)
    assert seq % tm == 0
    per = seq // tm
    return pl.pallas_call(
        _mm_resid_kernel,
        grid=(m // tm, n // tn),
        in_specs=[pl.BlockSpec((tm, k), lambda i, j: (i, 0)),
                  pl.BlockSpec((k, tn), lambda i, j: (0, j)),
                  pl.BlockSpec((tm, tn), lambda i, j: (i, j)),
                  pl.BlockSpec((None, 1, tn), lambda i, j: (i // per, 0, j))],
        out_specs=pl.BlockSpec((tm, tn), lambda i, j: (i, j)),
        out_shape=jax.ShapeDtypeStruct((m, n), F32),
        compiler_params=_params("parallel", "arbitrary"),
        name=name,
    )(x, w, resid, gate)


def _attn_a_kernel(q_ref, k_ref, v_ref, gq_ref, gk_ref, lam_ref, sub_ref, slope_ref, o_ref, kn_ref,
                   *, lam_init, tq, sq, seq):
    qi = pl.program_id(2)

    @pl.when(qi == 0)
    def _():
        kn_ref[...] = _head_norm(k_ref[...].astype(F32), gk_ref[...]).astype(BF16)

    lv = lam_ref[...]
    s01 = jnp.sum(lv[0:1] * lv[1:2], axis=-1, keepdims=True)
    s23 = jnp.sum(lv[2:3] * lv[3:4], axis=-1, keepdims=True)
    lam = jnp.exp(s01) - jnp.exp(s23) + lam_init

    v = v_ref[...]
    col = lax.broadcasted_iota(I32, (sq, seq), 1)
    for j in range(tq // sq):
        rs = slice(j * sq, (j + 1) * sq)
        qn = _head_norm(q_ref[rs, :].astype(F32), gq_ref[...])
        sc = _qk(_split_heads(qn), kn_ref[...])
        row = qi * tq + j * sq + lax.broadcasted_iota(I32, (sq, seq), 0)
        bias = slope_ref[...] * jnp.abs(row - col).astype(F32)

        def component(s):
            s = s - bias
            m = jnp.max(s, axis=-1, keepdims=True)
            p = jnp.exp(s - m)
            l = jnp.sum(p, axis=-1, keepdims=True)
            return jnp.dot(p.astype(BF16), v, preferred_element_type=F32) / l

        o = component(sc[:sq]) - lam * component(sc[sq:])
        ms = jnp.mean(o * o, axis=-1, keepdims=True)
        o_ref[rs, :] = (o * lax.rsqrt(ms + RMS_EPS) * sub_ref[...] * (1.0 - lam_init)).astype(o_ref.dtype)


def _attn_a(qkv, gq, gk, a_lambda, a_subln, lam_init):
    nb, seq, _ = qkv.shape
    tq, sq = 512, 256
    per = SEG_W // LANES
    slopes = jnp.asarray(_alibi_slopes(A_HEADS)).reshape(A_HEADS, 1, 1)
    kern = functools.partial(_attn_a_kernel, lam_init=lam_init, tq=tq, sq=sq, seq=seq)
    return pl.pallas_call(
        kern,
        grid=(nb, A_HEADS, seq // tq),
        in_specs=[pl.BlockSpec((None, tq, LANES), lambda b, h, i: (b, i, SEG_AQ * per + h)),
                  pl.BlockSpec((None, seq, LANES), lambda b, h, i: (b, 0, SEG_AK * per + h)),
                  pl.BlockSpec((None, seq, LANES), lambda b, h, i: (b, 0, SEG_AV * per + h)),
                  pl.BlockSpec((1, LANES), lambda b, h, i: (0, 0)),
                  pl.BlockSpec((1, LANES), lambda b, h, i: (0, 0)),
                  pl.BlockSpec((4, HEAD_DIM), lambda b, h, i: (0, 0)),
                  pl.BlockSpec((1, LANES), lambda b, h, i: (0, 0)),
                  pl.BlockSpec((None, 1, 1), lambda b, h, i: (h, 0, 0))],
        out_specs=pl.BlockSpec((None, tq, LANES), lambda b, h, i: (b, i, h)),
        out_shape=jax.ShapeDtypeStruct((nb, seq, BRANCH_W), BF16),
        scratch_shapes=[pltpu.VMEM((seq, LANES), BF16)],
        compiler_params=_params("parallel", "arbitrary", "arbitrary"),
        name="attn_a",
    )(qkv, qkv, qkv, gq, gk, a_lambda, a_subln.reshape(1, LANES), slopes)


def _attn_b_kernel(q_ref, k_ref, v_ref, gq_ref, gk_ref, slope_ref, sink_ref, o_ref, k2_ref, v2_ref,
                   *, tq, nk, seq):
    j = pl.program_id(1)
    qi = pl.program_id(2)

    @pl.when(qi == 0)
    def _():
        half = (lax.broadcasted_iota(I32, (seq, LANES), 1) >= HEAD_DIM).astype(I32)
        own = half == j
        kn = _head_norm(k_ref[...].astype(F32), gk_ref[...])
        k2_ref[...] = jnp.where(own, kn, pltpu.roll(kn, HEAD_DIM, axis=1)).astype(BF16)
        v = v_ref[...].astype(F32)
        v2_ref[...] = jnp.where(own, v, pltpu.roll(v, HEAD_DIM, axis=1)).astype(BF16)

    ws = pl.multiple_of(jnp.clip(qi * tq - B_WINDOW, 0, seq - nk), LANES)
    kw = k2_ref[pl.ds(ws, nk), :]
    vw = v2_ref[pl.ds(ws, nk), :]
    q = q_ref[...].astype(F32)
    g = gq_ref[...]
    qa = _head_norm(q[:, :LANES], g)
    qb = _head_norm(q[:, LANES:], g)
    q4 = jnp.concatenate([_split_heads(qa), _split_heads(qb)], axis=0)
    sc = _qk(q4, kw)
    row = qi * tq + lax.broadcasted_iota(I32, (tq, nk), 0)
    col = ws + lax.broadcasted_iota(I32, (tq, nk), 1)
    rel = jnp.abs(row - col)
    valid = rel <= B_WINDOW
    relf = rel.astype(F32)
    outs = []
    for gi in range(B_HEADS // B_KV_HEADS):
        s = jnp.where(valid, sc[gi * tq:(gi + 1) * tq] - slope_ref[gi] * relf, NEG_INF)
        m = jnp.max(s, axis=-1, keepdims=True)
        p = jnp.exp(s - m)
        l = jnp.sum(p, axis=-1, keepdims=True) + jnp.exp(sink_ref[gi] - m)
        outs.append(jnp.dot(p.astype(BF16), vw, preferred_element_type=F32) / l)
    lo = _lo_lanes((tq, LANES))
    o = jnp.concatenate([jnp.where(lo, outs[0], outs[1]), jnp.where(lo, outs[2], outs[3])], axis=1)
    o_ref[...] = o.astype(o_ref.dtype)


def _attn_b(qkv, kvb, gq, gk, b_sink):
    nb, seq, _ = qkv.shape
    tq = 128
    nk = tq + 2 * B_WINDOW
    grp = B_HEADS // B_KV_HEADS
    qw = grp * HEAD_DIM
    slopes = jnp.asarray(_alibi_slopes(B_HEADS)).reshape(B_HEADS, 1, 1)
    kern = functools.partial(_attn_b_kernel, tq=tq, nk=nk, seq=seq)
    return pl.pallas_call(
        kern,
        grid=(nb, B_KV_HEADS, seq // tq),
        in_specs=[pl.BlockSpec((None, tq, qw), lambda b, j, i: (b, i, SEG_BQ * (SEG_W // qw) + j)),
                  pl.BlockSpec((None, seq, LANES), lambda b, j, i: (b, 0, 0)),
                  pl.BlockSpec((None, seq, LANES), lambda b, j, i: (b, 0, 1)),
                  pl.BlockSpec((1, LANES), lambda b, j, i: (0, 0)),
                  pl.BlockSpec((1, LANES), lambda b, j, i: (0, 0)),
                  pl.BlockSpec((grp, 1, 1), lambda b, j, i: (j, 0, 0)),
                  pl.BlockSpec((grp, 1, 1), lambda b, j, i: (j, 0, 0))],
        out_specs=pl.BlockSpec((None, tq, qw), lambda b, j, i: (b, i, j)),
        out_shape=jax.ShapeDtypeStruct((nb, seq, BRANCH_W), BF16),
        scratch_shapes=[pltpu.VMEM((seq, LANES), BF16), pltpu.VMEM((seq, LANES), BF16)],
        compiler_params=_params("parallel", "arbitrary", "arbitrary"),
        name="attn_b",
    )(qkv, kvb, kvb, gq, gk, slopes, b_sink.reshape(B_HEADS, 1, 1))


def _attn_c_kernel(q_ref, k_ref, v_ref, gq_ref, gk_ref, slope_ref, o_ref, lse_ref, kn_ref,
                   *, tq, nk, sub, side, dil):
    qi = pl.program_id(2)
    n_pairs = SEG_W // LANES

    @pl.when(qi == 0)
    def _():
        for hp in range(n_pairs):
            cs = slice(hp * LANES, (hp + 1) * LANES)
            kn_ref[:, cs] = _head_norm(k_ref[:, cs].astype(F32), gk_ref[...]).astype(BF16)

    ws = pl.multiple_of(jnp.clip(qi * tq - side, 0, sub - nk), HEAD_DIM)
    row = qi * tq + lax.broadcasted_iota(I32, (tq, nk), 0)
    col = ws + lax.broadcasted_iota(I32, (tq, nk), 1)
    rel = jnp.abs(row - col)
    valid = rel <= side
    relf = (dil * rel).astype(F32)
    lo = _lo_lanes((tq, LANES))
    for hp in range(n_pairs):
        cs = slice(hp * LANES, (hp + 1) * LANES)
        qn = _head_norm(q_ref[:, cs].astype(F32), gq_ref[...])
        sc = _qk(_split_heads(qn), kn_ref[pl.ds(ws, nk), cs])
        vw = v_ref[pl.ds(ws, nk), cs]
        o2, l2 = [], []
        for t in range(2):
            s = jnp.where(valid, sc[t * tq:(t + 1) * tq] - slope_ref[2 * hp + t] * relf, NEG_INF)
            m = jnp.max(s, axis=-1, keepdims=True)
            p = jnp.exp(s - m)
            l = jnp.sum(p, axis=-1, keepdims=True)
            o2.append(jnp.dot(p.astype(BF16), vw, preferred_element_type=F32) / l)
            l2.append(m + jnp.log(l))
        o_ref[:, cs] = jnp.where(lo, o2[0], o2[1]).astype(o_ref.dtype)
        lse_ref[:, cs] = jnp.where(lo, l2[0], l2[1])


def _attn_c_group(src, seg, gi, gq, gk):
    nb, dil, sub, _ = src.shape
    win, gdil = C_DILATIONS[gi]
    assert gdil == dil
    side = win // (2 * dil)
    tq = min(256, sub)
    nk = min(tq + 2 * side, sub)
    slopes = jnp.asarray(_alibi_slopes(N_DIL * C_HEADS).reshape(N_DIL, C_HEADS)[gi]).reshape(C_HEADS, 1, 1)
    kern = functools.partial(_attn_c_kernel, tq=tq, nk=nk, sub=sub, side=side, dil=dil)
    return pl.pallas_call(
        kern,
        grid=(nb, dil, sub // tq),
        in_specs=[pl.BlockSpec((None, None, tq, SEG_W), lambda b, r, i: (b, r, i, seg)),
                  pl.BlockSpec((None, None, sub, SEG_W), lambda b, r, i: (b, r, 0, seg + 1)),
                  pl.BlockSpec((None, None, sub, SEG_W), lambda b, r, i: (b, r, 0, seg + 2)),
                  pl.BlockSpec((1, LANES), lambda b, r, i: (0, 0)),
                  pl.BlockSpec((1, LANES), lambda b, r, i: (0, 0)),
                  pl.BlockSpec((C_HEADS, 1, 1), lambda b, r, i: (0, 0, 0))],
        out_specs=[pl.BlockSpec((None, None, tq, SEG_W), lambda b, r, i: (b, r, i, 0)),
                   pl.BlockSpec((None, None, tq, SEG_W), lambda b, r, i: (b, r, i, 0))],
        out_shape=[jax.ShapeDtypeStruct((nb, dil, sub, SEG_W), BF16),
                   jax.ShapeDtypeStruct((nb, dil, sub, SEG_W), F32)],
        scratch_shapes=[pltpu.VMEM((sub, SEG_W), BF16)],
        compiler_params=_params("parallel", "arbitrary", "arbitrary"),
        name=f"attn_c{gi}",
    )(src, src, src, gq, gk, slopes)


def _attn_d_kernel(q_ref, k_ref, v_ref, gq_ref, gk_ref, bias_ref, o_ref, kn_ref, *, rows, nk, rb):
    blk = pl.program_id(1)
    n_pairs = SEG_W // LANES

    @pl.when(blk == 0)
    def _():
        for hp in range(n_pairs):
            cs = slice(hp * LANES, (hp + 1) * LANES)
            kn_ref[:, cs] = _head_norm(k_ref[:, cs].astype(F32), gk_ref[...]).astype(BF16)

    kr = nk // GRID_W
    lo = _lo_lanes((GRID_W, LANES))
    for rr in range(rb):
        r = blk * rb + rr
        kr0 = jnp.clip(r - kr // 2, 0, rows - kr)
        d0 = kr0 - r + NA_ROWS - 1
        ws = pl.multiple_of(kr0 * GRID_W, GRID_W)
        rs = slice(rr * GRID_W, (rr + 1) * GRID_W)
        for hp in range(n_pairs):
            cs = slice(hp * LANES, (hp + 1) * LANES)
            qn = _head_norm(q_ref[rs, cs].astype(F32), gq_ref[...])
            sc = _qk(_split_heads(qn), kn_ref[pl.ds(ws, nk), cs])
            vw = v_ref[pl.ds(ws, nk), cs]
            o2 = []
            for t in range(2):
                s = sc[t * GRID_W:(t + 1) * GRID_W] + bias_ref[d0, 2 * hp + t]
                m = jnp.max(s, axis=-1, keepdims=True)
                p = jnp.exp(s - m)
                l = jnp.sum(p, axis=-1, keepdims=True)
                o2.append(jnp.dot(p.astype(BF16), vw, preferred_element_type=F32) / l)
            o_ref[rs, cs] = jnp.where(lo, o2[0], o2[1]).astype(o_ref.dtype)


def _na_bias_table(rpb, rows):
    kr = min(NA_ROWS, rows)
    qc = np.arange(GRID_W)[:, None]
    kc = np.arange(GRID_W)[None, :]
    col_start = np.clip(qc - NA_COLS // 2, 0, GRID_W - NA_COLS)
    ok = (kc >= col_start) & (kc < col_start + NA_COLS)
    dc = np.clip(kc - qc + NA_COLS - 1, 0, 2 * NA_COLS - 2)
    drow = np.arange(kr)[:, None] + np.arange(kr)[None, :]
    t = rpb[:, drow][:, :, :, dc]
    t = jnp.where(ok[None, None, None], t.astype(F32), NEG_INF)
    t = t.transpose(1, 0, 3, 2, 4)
    return t.reshape(kr, D_HEADS, GRID_W, kr * GRID_W)


def _attn_d(qkv, gq, gk, d_rpb):
    nb, seq, _ = qkv.shape
    rows = seq // GRID_W
    kr = min(NA_ROWS, rows)
    nk = kr * GRID_W
    table = _na_bias_table(d_rpb, rows)
    rb = 4
    assert rows % rb == 0
    kern = functools.partial(_attn_d_kernel, rows=rows, nk=nk, rb=rb)
    return pl.pallas_call(
        kern,
        grid=(nb, rows // rb),
        in_specs=[pl.BlockSpec((None, rb * GRID_W, SEG_W), lambda b, r: (b, r, SEG_DQ)),
                  pl.BlockSpec((None, seq, SEG_W), lambda b, r: (b, 0, SEG_DK)),
                  pl.BlockSpec((None, seq, SEG_W), lambda b, r: (b, 0, SEG_DV)),
                  pl.BlockSpec((1, LANES), lambda b, r: (0, 0)),
                  pl.BlockSpec((1, LANES), lambda b, r: (0, 0)),
                  pl.BlockSpec((kr, D_HEADS, GRID_W, nk), lambda b, r: (0, 0, 0, 0))],
        out_specs=pl.BlockSpec((None, rb * GRID_W, SEG_W), lambda b, r: (b, r, 0)),
        out_shape=jax.ShapeDtypeStruct((nb, seq, BRANCH_W), BF16),
        scratch_shapes=[pltpu.VMEM((seq, SEG_W), BF16)],
        compiler_params=_params("parallel", "arbitrary"),
        name="attn_d",
    )(qkv, qkv, qkv, gq, gk, table)


def _to_token_order(blk_ref, scr_ref):
    dil, n, _ = blk_ref.shape
    for r in range(dil):
        for c in range(SEG_W // LANES):
            scr_ref[c, pl.ds(r, n, stride=dil), :] = blk_ref[r, :, c * LANES:(c + 1) * LANES].astype(F32)


def _merge_kernel(oa_ref, ob_ref, oc0_ref, oc1_ref, oc2_ref, l0_ref, l1_ref, l2_ref, od_ref,
                  gates_ref, wb_ref, o_ref, s_o1, s_l1, s_o2, s_l2):
    _to_token_order(oc1_ref, s_o1)
    _to_token_order(l1_ref, s_l1)
    _to_token_order(oc2_ref, s_o2)
    _to_token_order(l2_ref, s_l2)
    slabs = []
    for c in range(SEG_W // LANES):
        cs = slice(c * LANES, (c + 1) * LANES)
        l0, l1, l2 = l0_ref[:, cs], s_l1[c], s_l2[c]
        lm = jnp.maximum(jnp.maximum(l0, l1), l2)
        e0, e1, e2 = jnp.exp(l0 - lm), jnp.exp(l1 - lm), jnp.exp(l2 - lm)
        oc = (e0 * oc0_ref[:, cs].astype(F32) + e1 * s_o1[c] + e2 * s_o2[c]) / (e0 + e1 + e2)
        slabs.append(oc.astype(BF16))
    branches = (oa_ref[...], ob_ref[...], jnp.concatenate(slabs, axis=1), od_ref[...])
    acc = None
    for n, br in enumerate(branches):
        proj = jnp.dot(br, wb_ref[n], preferred_element_type=F32)
        term = gates_ref[:, n * D_MODEL:(n + 1) * D_MODEL].astype(F32) * proj
        acc = term if acc is None else acc + term
    o_ref[...] = acc.astype(o_ref.dtype)


def _merge(oa, ob, oc, lses, od, gates, w_branch):
    nb, seq, _ = oa.shape
    tm = 256
    per = seq // tm
    tok = pl.BlockSpec((None, tm, BRANCH_W), lambda b, i: (b, i, 0))

    def res(dil):
        return pl.BlockSpec((None, dil, tm // dil, BRANCH_W), lambda b, i: (b, 0, i, 0))

    dils = [d for _, d in C_DILATIONS]
    assert dils[0] == 1
    res0 = pl.BlockSpec((None, None, tm, BRANCH_W), lambda b, i: (b, 0, i, 0))
    slab = pltpu.VMEM((BRANCH_W // LANES, tm, LANES), F32)
    return pl.pallas_call(
        _merge_kernel,
        grid=(nb, per),
        in_specs=[tok, tok, res0, res(dils[1]), res(dils[2]), res0, res(dils[1]), res(dils[2]), tok,
                  pl.BlockSpec((tm, N_BRANCH * D_MODEL), lambda b, i: (b * per + i, 0)),
                  pl.BlockSpec((N_BRANCH, BRANCH_W, D_MODEL), lambda b, i: (0, 0, 0))],
        out_specs=pl.BlockSpec((tm, D_MODEL), lambda b, i: (b * per + i, 0)),
        out_shape=jax.ShapeDtypeStruct((nb * seq, D_MODEL), BF16),
        scratch_shapes=[slab, slab, slab, slab],
        compiler_params=_params("parallel", "arbitrary"),
        name="merge",
    )(oa, ob, oc[0], oc[1], oc[2], lses[0], lses[1], lses[2], od, gates, w_branch)


def _route_kernel(x_ref, g_ref, sc_ref, sh_ref, wr_ref, br_ref, hf_ref, idx_ref, p_ref):
    h = _modulated_norm(x_ref[...], g_ref, sc_ref, sh_ref)
    _store_row_tiles(hf_ref, h)
    logits = jnp.dot(h.astype(BF16), wr_ref[...], preferred_element_type=F32) + br_ref[...]
    lane = lax.broadcasted_iota(I32, logits.shape, 1)
    idx_t = jnp.zeros(logits.shape, I32)
    e_t = jnp.zeros(logits.shape, F32)
    den = None
    top = None
    for k in range(TOP_K):
        m = jnp.max(logits, axis=-1, keepdims=True)
        idx = jnp.min(jnp.where(logits == m, lane, LANES), axis=-1, keepdims=True)
        logits = jnp.where(lane == idx, -jnp.inf, logits)
        if k == 0:
            top = m
        e = jnp.exp(m - top)
        den = e if den is None else den + e
        idx_t = jnp.where(lane == k, idx, idx_t)
        e_t = jnp.where(lane == k, e, e_t)
    idx_ref[...] = idx_t
    p_ref[...] = e_t / den


def _route(x, g, sc, sh, w_router, b_router):
    nb, s, d = x.shape
    tm = 256
    wr = jnp.zeros((d, LANES), BF16).at[:, :N_EXPERTS].set(w_router.astype(BF16))
    br = jnp.full((1, LANES), NEG_INF, F32).at[0, :N_EXPERTS].set(b_router)
    per = s // tm
    return pl.pallas_call(
        _route_kernel,
        grid=(nb, per),
        in_specs=[pl.BlockSpec((None, tm, d), lambda b, i: (b, i, 0)),
                  pl.BlockSpec((1, d), lambda b, i: (0, 0)),
                  pl.BlockSpec((None, 1, d), lambda b, i: (b, 0, 0)),
                  pl.BlockSpec((None, 1, d), lambda b, i: (b, 0, 0)),
                  pl.BlockSpec((d, LANES), lambda b, i: (0, 0)),
                  pl.BlockSpec((1, LANES), lambda b, i: (0, 0))],
        out_specs=[pl.BlockSpec((tm * ROW_TILE, LANES), lambda b, i: (b * per + i, 0)),
                   pl.BlockSpec((tm, LANES), lambda b, i: (b * per + i, 0)),
                   pl.BlockSpec((tm, LANES), lambda b, i: (b * per + i, 0))],
        out_shape=[jax.ShapeDtypeStruct((nb * s * ROW_TILE, LANES), U32),
                   jax.ShapeDtypeStruct((nb * s, LANES), I32),
                   jax.ShapeDtypeStruct((nb * s, LANES), F32)],
        compiler_params=_params("parallel", "parallel"),
        name="route",
    )(x, g.reshape(1, d), sc, sh, wr, br)


def _tile_rows(t):
    return pl.ds(pl.multiple_of(t * ROW_TILE, ROW_TILE), ROW_TILE)


def _dispatch_kernel(dest_ref, src_ref, init_ref, o_ref, sem, *, rows):
    del init_ref

    def row_copy(t, d):
        return pltpu.make_async_copy(src_ref.at[_tile_rows(t), :], o_ref.at[_tile_rows(d), :], sem)

    def issue(t, c):
        for k in range(TOP_K):
            row_copy(t, dest_ref[0, 0, t * TOP_K + k]).start(priority=k % DMA_QUEUES)
        return c
    lax.fori_loop(0, rows, issue, 0, unroll=DMA_UNROLL // TOP_K)

    def drain(t, c):
        for k in range(TOP_K):
            row_copy(t, 0).wait()
        return c
    lax.fori_loop(0, rows, drain, 0, unroll=DMA_UNROLL // TOP_K)


def _dispatch_rows(src, dest, n_slots):
    rows = DISPATCH_ROWS
    n_tok = src.shape[0] // ROW_TILE
    nblk = n_tok // rows
    kern = functools.partial(_dispatch_kernel, rows=rows)
    return pl.pallas_call(
        kern,
        grid=(nblk,),
        in_specs=[pl.BlockSpec((1, 1, rows * TOP_K), lambda i: (i, 0, 0), memory_space=pltpu.SMEM),
                  pl.BlockSpec((rows * ROW_TILE, LANES), lambda i: (i, 0)),
                  pl.BlockSpec(memory_space=pl.ANY)],
        out_specs=pl.BlockSpec(memory_space=pl.ANY),
        out_shape=jax.ShapeDtypeStruct((n_slots * ROW_TILE, LANES), U32),
        scratch_shapes=[pltpu.SemaphoreType.DMA(())],
        input_output_aliases={2: 0},
        compiler_params=_params("arbitrary"),
        name="moe_dispatch",
    )(dest.reshape(nblk, 1, rows * TOP_K), src, jnp.zeros((n_slots * ROW_TILE, LANES), U32))


def _expert_kernel(be_ref, nu_ref, xs_ref, wg_ref, wu_ref, wdl_ref, wdh_ref, bg_ref, bu_ref, bdl_ref, bdh_ref,
                   o_ref, xb_ref, a_ref, *, n_up, n_down):
    i = pl.program_id(0)
    f = pl.program_id(1)
    used = i < nu_ref[0]
    rows = xb_ref.shape[0]
    tf = wg_ref.shape[1]

    @pl.when(used & (f == 0))
    def _():
        for c in range(ROW_TILE):
            lo, hi = _load_row_tile_chunk(xs_ref, c)
            xb_ref[:, c * LANES:(c + 1) * LANES] = lo.astype(BF16)
            xb_ref[:, HALF_D + c * LANES:HALF_D + (c + 1) * LANES] = hi.astype(BF16)

    halves = [slice(h * (rows // 2), (h + 1) * (rows // 2)) for h in range(2)]

    @pl.when(used & (f < n_up))
    def _():
        wg = wg_ref[...].astype(BF16)
        wu = wu_ref[...].astype(BF16)
        acts = []
        for rs in halves:
            x = xb_ref[rs, :]
            g = jnp.dot(x, wg, preferred_element_type=F32) + bg_ref[...]
            u = jnp.dot(x, wu, preferred_element_type=F32) + bu_ref[...]
            g = jnp.minimum(g, SWIGLU_LIMIT)
            u = jnp.clip(u, -SWIGLU_LIMIT, SWIGLU_LIMIT)
            acts.append(((u + 1.0) * g * jax.nn.sigmoid(SWIGLU_ALPHA * g)).astype(BF16))
        for ff in range(n_up):
            @pl.when(f == ff)
            def _(ff=ff):
                for rs, a in zip(halves, acts):
                    a_ref[rs, ff * tf:(ff + 1) * tf] = a

    @pl.when(used & (f >= n_up))
    def _():
        m = f - n_up
        wdl = wdl_ref[...].astype(BF16)
        wdh = wdh_ref[...].astype(BF16)
        per = wdl.shape[1] // LANES
        for h, rs in enumerate(halves):
            a = a_ref[rs, :]
            lo = jnp.dot(a, wdl, preferred_element_type=F32) + bdl_ref[...]
            hi = jnp.dot(a, wdh, preferred_element_type=F32) + bdh_ref[...]
            packed = _pack_pair(lo, hi)
            for j in range(per):
                start = h * (rows // 2) * ROW_TILE + m * per + j
                o_ref[pl.ds(start, rows // 2, stride=ROW_TILE), :] = packed[:, j * LANES:(j + 1) * LANES]

    @pl.when(jnp.logical_not(used) & (f == n_up + n_down - 1))
    def _():
        o_ref[...] = jnp.zeros(o_ref.shape, o_ref.dtype)


def _experts(xs, block_expert, n_used, layer, w_gate_up, b_gate_up, w_down, b_down):
    rows, tf, tn = MOE_ROWS, MOE_TF, MOE_TN
    n_slots = xs.shape[0] // ROW_TILE
    nblk = n_slots // rows
    n_up = D_FF // tf
    n_down = HALF_D // tn
    depth, e = w_gate_up.shape[:2]
    kern = functools.partial(_expert_kernel, n_up=n_up, n_down=n_down)

    def up(i, f, nu):
        return jnp.where(i < nu[0], jnp.minimum(f, n_up - 1), n_up - 1)

    def down(i, f, nu):
        return jnp.where((i < nu[0]) & (f >= n_up // 2), jnp.maximum(f - n_up, 0), n_down - 1)

    return pl.pallas_call(
        kern,
        grid_spec=pltpu.PrefetchScalarGridSpec(
            num_scalar_prefetch=2,
            grid=(nblk, n_up + n_down),
            in_specs=[pl.BlockSpec((rows * ROW_TILE, LANES), lambda i, f, be, nu: (i, 0)),
                      pl.BlockSpec((None, None, D_MODEL, tf), lambda i, f, be, nu: (layer, be[i], 0, up(i, f, nu))),
                      pl.BlockSpec((None, None, D_MODEL, tf),
                                   lambda i, f, be, nu: (layer, be[i], 0, n_up + up(i, f, nu))),
                      pl.BlockSpec((None, None, D_FF, tn), lambda i, f, be, nu: (layer, be[i], 0, down(i, f, nu))),
                      pl.BlockSpec((None, None, D_FF, tn),
                                   lambda i, f, be, nu: (layer, be[i], 0, n_down + down(i, f, nu))),
                      pl.BlockSpec((None, None, 1, tf), lambda i, f, be, nu: (layer, be[i], 0, up(i, f, nu))),
                      pl.BlockSpec((None, None, 1, tf),
                                   lambda i, f, be, nu: (layer, be[i], 0, n_up + up(i, f, nu))),
                      pl.BlockSpec((None, None, 1, tn), lambda i, f, be, nu: (layer, be[i], 0, down(i, f, nu))),
                      pl.BlockSpec((None, None, 1, tn),
                                   lambda i, f, be, nu: (layer, be[i], 0, n_down + down(i, f, nu)))],
            out_specs=pl.BlockSpec((rows * ROW_TILE, LANES), lambda i, f, be, nu: (i, 0)),
            scratch_shapes=[pltpu.VMEM((rows, D_MODEL), BF16), pltpu.VMEM((rows, D_FF), BF16)],
        ),
        out_shape=jax.ShapeDtypeStruct((n_slots * ROW_TILE, LANES), U32),
        compiler_params=_params("arbitrary", "arbitrary"),
        name="moe_experts",
    )(block_expert, n_used, xs, w_gate_up, w_gate_up, w_down, w_down,
      b_gate_up.reshape(depth, e, 1, 2 * D_FF), b_gate_up.reshape(depth, e, 1, 2 * D_FF),
      b_down.reshape(depth, e, 1, D_MODEL), b_down.reshape(depth, e, 1, D_MODEL))


def _combine_kernel(dest_ref, src_ref, p_ref, x_ref, g_ref, o_ref, buf_ref, sem, *, rows):
    def row_copy(k, t, d):
        return pltpu.make_async_copy(src_ref.at[_tile_rows(d), :], buf_ref.at[k, _tile_rows(t), :], sem)

    for k in range(TOP_K):
        def issue(t2, c, k=k):
            for par in range(DMA_QUEUES):
                t = t2 * DMA_QUEUES + par
                row_copy(k, t, dest_ref[0, 0, t * TOP_K + k]).start(priority=par)
            return c
        lax.fori_loop(0, rows // DMA_QUEUES, issue, 0, unroll=DMA_UNROLL // DMA_QUEUES)
    for k in range(TOP_K):
        def drain(t, c, k=k):
            row_copy(k, t, 0).wait()
            return c
        lax.fori_loop(0, rows, drain, 0, unroll=DMA_UNROLL)

    p = p_ref[...]
    pk = [p[:, k:k + 1] for k in range(TOP_K)]
    for c in range(ROW_TILE):
        y_lo = None
        y_hi = None
        for k in range(TOP_K):
            lo, hi = _load_row_tile_chunk(buf_ref.at[k], c)
            y_lo = pk[k] * lo if y_lo is None else y_lo + pk[k] * lo
            y_hi = pk[k] * hi if y_hi is None else y_hi + pk[k] * hi
        cl = slice(c * LANES, (c + 1) * LANES)
        ch = slice(HALF_D + c * LANES, HALF_D + (c + 1) * LANES)
        o_ref[:, cl] = x_ref[:, cl] + g_ref[:, cl] * y_lo
        o_ref[:, ch] = x_ref[:, ch] + g_ref[:, ch] * y_hi


def _combine(outs, dest, probs, x, gate, seq):
    t, d = x.shape
    rows = COMBINE_ROWS
    nblk = t // rows
    per = seq // rows
    kern = functools.partial(_combine_kernel, rows=rows)
    return pl.pallas_call(
        kern,
        grid=(nblk,),
        in_specs=[pl.BlockSpec((1, 1, rows * TOP_K), lambda i: (i, 0, 0), memory_space=pltpu.SMEM),
                  pl.BlockSpec(memory_space=pl.ANY),
                  pl.BlockSpec((rows, LANES), lambda i: (i, 0)),
                  pl.BlockSpec((rows, d), lambda i: (i, 0)),
                  pl.BlockSpec((None, 1, d), lambda i: (i // per, 0, 0))],
        out_specs=pl.BlockSpec((rows, d), lambda i: (i, 0)),
        out_shape=jax.ShapeDtypeStruct((t, d), F32),
        scratch_shapes=[pltpu.VMEM((TOP_K, rows * ROW_TILE, LANES), U32), pltpu.SemaphoreType.DMA(())],
        compiler_params=_params("arbitrary"),
        name="moe_combine",
    )(dest.reshape(nblk, 1, rows * TOP_K), outs, probs, x, gate)


def _routing_plan(top_idx, n_tok):
    expert = top_idx.reshape(-1)
    onehot = jax.nn.one_hot(expert, N_EXPERTS, dtype=I32)
    csum = jnp.cumsum(onehot, axis=0)
    rank = jnp.take_along_axis(csum, expert[:, None], axis=1)[:, 0] - 1
    padded = (csum[-1] + MOE_ROWS - 1) // MOE_ROWS * MOE_ROWS
    pad_end = jnp.cumsum(padded)
    dest = (pad_end[expert] - padded[expert] + rank).astype(I32)
    n_blocks = -(-(n_tok * TOP_K) // MOE_ROWS) + N_EXPERTS
    block_expert = jnp.minimum(
        jnp.searchsorted(pad_end, jnp.arange(n_blocks, dtype=I32) * MOE_ROWS, side='right'),
        N_EXPERTS - 1).astype(I32)
    n_blocks_used = (pad_end[-1:] // MOE_ROWS).astype(I32)
    return dest, block_expert, n_blocks_used, n_blocks * MOE_ROWS


def _moe(x, g, sc, sh, gate, layer, w_router, b_router, w_gate_up, b_gate_up, w_down, b_down):
    nb, seq, d = x.shape
    n_tok = nb * seq
    hf, idx_t, p_t = _route(x, g, sc, sh, w_router, b_router)
    dest, block_expert, n_blocks_used, n_slots = _routing_plan(idx_t[:, :TOP_K], n_tok)
    xs = _dispatch_rows(hf, dest, n_slots)
    outs = _experts(xs, block_expert, n_blocks_used, layer, w_gate_up, b_gate_up, w_down, b_down)
    y = _combine(outs, dest, p_t, x.reshape(n_tok, d), gate, seq)
    return y.reshape(nb, seq, d)


def _pair_gain(g, scale=1.0):
    return (jnp.tile(g.astype(F32), LANES // HEAD_DIM) * scale).reshape(1, LANES)


def _token_mixer(x, layer, lam_init, g, sc, sh, gate, p):
    nb, seq, d = x.shape
    n_tok = nb * seq
    hm = _normmod(x, g, sc, sh).reshape(n_tok, d)
    w_in = p['w_in'][layer]
    w_main = jnp.concatenate([w_in[:, :BKV_LO], w_in[:, C_LO:C_LO + C_GROUP_W], w_in[:, D_LO:]],
                             axis=1).astype(BF16)
    qkv = _matmul(hm, w_main, "in_proj").reshape(nb, seq, QKV_W)
    kvb = _matmul(hm, w_in[:, BKV_LO:BKV_HI].astype(BF16), "in_proj_bkv").reshape(nb, seq, BKV_HI - BKV_LO)
    c_src = [(qkv.reshape(nb, 1, seq, QKV_W), SEG_C0)]
    for gi in range(1, N_DIL):
        lo = C_LO + gi * C_GROUP_W
        c_src.append((_matmul_by_residue(hm, w_in[:, lo:lo + C_GROUP_W].astype(BF16), nb, seq,
                                         C_DILATIONS[gi][1], f"in_proj_c{gi}"), 0))
    qs = HEAD_DIM ** -0.5
    oa = _attn_a(qkv, _pair_gain(p['a_qk_norm'][layer, 0], qs), _pair_gain(p['a_qk_norm'][layer, 1]),
                 p['a_lambda'][layer], p['a_subln'][layer], lam_init)
    ob = _attn_b(qkv, kvb, _pair_gain(p['b_qk_norm'][layer, 0], qs), _pair_gain(p['b_qk_norm'][layer, 1]),
                 p['b_sink'][layer])
    ocs, lses = [], []
    for gi in range(N_DIL):
        o, lse = _attn_c_group(c_src[gi][0], c_src[gi][1], gi, _pair_gain(p['c_qk_norm'][layer, gi, 0], qs),
                               _pair_gain(p['c_qk_norm'][layer, gi, 1]))
        ocs.append(o)
        lses.append(lse)
    od = _attn_d(qkv, _pair_gain(p['d_qk_norm'][layer, 0], qs), _pair_gain(p['d_qk_norm'][layer, 1]),
                 p['d_rpb'][layer])
    gates = _matmul_sigmoid(hm, p['w_gate'][layer].astype(BF16), p['b_gate'][layer], "gate_proj")
    mixed = _merge(oa, ob, ocs, lses, od, gates, p['w_branch'][layer].astype(BF16))
    y = _matmul_residual(mixed, p['w_out'][layer].astype(BF16), x.reshape(n_tok, d), gate, seq, "out_proj")
    return y.reshape(nb, seq, d)


def kernel(x_prompt, x_sample, c_prompt, c_sample, w_ada, b_ada, norm_g, w_in, a_qk_norm, a_lambda, a_subln, b_qk_norm, b_sink, c_qk_norm, d_qk_norm, d_rpb, w_gate, b_gate, w_branch, w_out, w_router, b_router, w_gate_up, b_gate_up, w_down, b_down):
    p = dict(w_in=w_in, a_qk_norm=a_qk_norm, a_lambda=a_lambda, a_subln=a_subln, b_qk_norm=b_qk_norm,
             b_sink=b_sink, c_qk_norm=c_qk_norm, d_qk_norm=d_qk_norm, d_rpb=d_rpb, w_gate=w_gate,
             b_gate=b_gate, w_branch=w_branch, w_out=w_out)
    assert x_prompt.shape[1:] == x_sample.shape[1:]
    n_prompt = x_prompt.shape[0]
    x = jnp.concatenate([x_prompt, x_sample], axis=0)
    c = jnp.concatenate([c_prompt, c_sample], axis=0)
    nb, seq, d = x.shape
    depth = w_ada.shape[0]
    mod = _ada_mod(c, w_ada, b_ada).reshape(depth, nb, 6, 1, d)
    for layer in range(depth):
        sh1, sc1, g1, sh2, sc2, g2 = (mod[layer, :, i] for i in range(6))
        lam_init = 0.8 - 0.6 * math.exp(-0.3 * layer)
        x = _token_mixer(x, layer, lam_init, norm_g[layer, 0], sc1, sh1, g1, p)
        x = _moe(x, norm_g[layer, 1], sc2, sh2, g2, layer, w_router[layer], b_router[layer],
                 w_gate_up, b_gate_up, w_down, b_down)
    return x[:n_prompt], x[n_prompt:]
```
